```python
import functools
import jax
import jax.numpy as jnp
from jax import lax
import numpy as np

D_MODEL = 1024
BATCH = 2
SEQ = 8192
DEPTH = 2
DEC_BATCH = 128
DEC_SEQ = 8
PAST_LEN = 2048
PAGE_SIZE = 128

FOX_HEADS = 8
FOX_HEAD_DIM = 64
FOX_WIDTH = FOX_HEADS * FOX_HEAD_DIM
Q_BLOCK = 128
GM_GROUPS = 4
GM_GROUP_DIM = 128
GM_WIDTH = GM_GROUPS * GM_GROUP_DIM
CHUNK = 128
MEM_LEN = 256
MEM_HEADS = 4
MEM_HEAD_DIM = 128
MEM_WIDTH = MEM_HEADS * MEM_HEAD_DIM
N_BRANCH = 3
BRANCH_WIDTH = 512
IN_WIDTH = 3 * FOX_WIDTH + FOX_HEADS + 2 * GM_WIDTH + MEM_WIDTH
SPLITS = (FOX_WIDTH, 2 * FOX_WIDTH, 3 * FOX_WIDTH, 3 * FOX_WIDTH + FOX_HEADS,
          3 * FOX_WIDTH + FOX_HEADS + GM_WIDTH, 3 * FOX_WIDTH + FOX_HEADS + 2 * GM_WIDTH)
N_GROUPS = 4
EXPERTS_PER_GROUP = 8
N_EXPERTS = N_GROUPS * EXPERTS_PER_GROUP
TOP_K = 2
EXPERT_FF = 512
EPS = 1e-6

kernel_name = 'fox_gmlp_memxattn_hmoe_step'


def rmsnorm(x, g):
    xf = x.astype(jnp.float32)
    y = xf * lax.rsqrt(jnp.mean(xf * xf, axis=-1, keepdims=True) + EPS)
    return (y * g.astype(jnp.float32)).astype(x.dtype)


def project_in(h, w_in, b_f):
    z = h @ w_in
    q, k, v, f, u, gv, mq = jnp.split(z, SPLITS, axis=-1)
    lead = h.shape[:-1]
    q = q.reshape(lead + (FOX_HEADS, FOX_HEAD_DIM))
    k = k.reshape(lead + (FOX_HEADS, FOX_HEAD_DIM))
    v = v.reshape(lead + (FOX_HEADS, FOX_HEAD_DIM))
    logf = jax.nn.log_sigmoid(f.astype(jnp.float32) + b_f.astype(jnp.float32))
    mq = mq.reshape(lead + (MEM_HEADS, MEM_HEAD_DIM))
    return q, k, v, logf, u, gv, mq


def fox_prompt(q, k, v, logf):
    b, s, h, d = q.shape
    nb = s // Q_BLOCK
    F = jnp.cumsum(logf, axis=1).transpose(0, 2, 1)
    qb = q.reshape(b, nb, Q_BLOCK, h, d).swapaxes(0, 1)
    Fb = F.reshape(b, h, nb, Q_BLOCK).transpose(2, 0, 1, 3)
    kpos = jnp.arange(s)
    scale = FOX_HEAD_DIM ** -0.5

    def block(args):
        qi, Fi, i = args
        sc = jnp.einsum('bqhd,bkhd->bhqk', qi, k).astype(jnp.float32) * scale
        sc = sc + (Fi[..., :, None] - F[:, :, None, :])
        qpos = i * Q_BLOCK + jnp.arange(Q_BLOCK)
        sc = jnp.where(kpos[None, :] <= qpos[:, None], sc, -jnp.inf)
        p = jax.nn.softmax(sc, axis=-1).astype(v.dtype)
        return jnp.einsum('bhqk,bkhd->bqhd', p, v)

    out = lax.map(block, (qb, Fb, jnp.arange(nb)))
    return out.swapaxes(0, 1).reshape(b, s, h * d)


def fox_sample(q, k, v, logf, k_past, v_past, logf_past):
    b, t = q.shape[0], q.shape[1]
    p_len = k_past.shape[1]
    scale = FOX_HEAD_DIM ** -0.5
    Fp = jnp.cumsum(logf_past.astype(jnp.float32), axis=1)
    Fn = Fp[:, -1:, :] + jnp.cumsum(logf, axis=1)
    Fp = Fp.transpose(0, 2, 1)
    Fn = Fn.transpose(0, 2, 1)
    sp = jnp.einsum('bthd,bshd->bhts', q, k_past).astype(jnp.float32) * scale
    sp = sp + (Fn[..., :, None] - Fp[:, :, None, :])
    sn = jnp.einsum('bthd,bshd->bhts', q, k).astype(jnp.float32) * scale
    sn = sn + (Fn[..., :, None] - Fn[:, :, None, :])
    sn = jnp.where(jnp.tril(jnp.ones((t, t), dtype=bool)), sn, -jnp.inf)
    pr = jax.nn.softmax(jnp.concatenate([sp, sn], axis=-1), axis=-1).astype(v.dtype)
    out = (jnp.einsum('bhts,bshd->bthd', pr[..., :p_len], v_past)
           + jnp.einsum('bhts,bshd->bthd', pr[..., p_len:], v))
    return out.reshape(b, t, FOX_WIDTH)


def mem_kv(mem, g, w):
    kv = rmsnorm(mem, g) @ w
    mk, mv = jnp.split(kv, 2, axis=-1)
    shp = mem.shape[:-1] + (MEM_HEADS, MEM_HEAD_DIM)
    return mk.reshape(shp), mv.reshape(shp)


def mem_attend(q, mk, mv):
    sc = jnp.einsum('bthd,bmhd->bhtm', q, mk).astype(jnp.float32) * (MEM_HEAD_DIM ** -0.5)
    p = jax.nn.softmax(sc, axis=-1).astype(mv.dtype)
    out = jnp.einsum('bhtm,bmhd->bthd', p, mv)
    return out.reshape(q.shape[:2] + (MEM_WIDTH,))


def gmlp_front(u, gv, g):
    u = jax.nn.gelu(u)
    vf = jax.nn.gelu(gv).astype(jnp.float32)
    mu = jnp.mean(vf, axis=-1, keepdims=True)
    var = jnp.mean(jnp.square(vf - mu), axis=-1, keepdims=True)
    vn = ((vf - mu) * lax.rsqrt(var + EPS) * g.astype(jnp.float32)).astype(gv.dtype)
    return u, vn


def spatial_gate(u, vn, ws, bs):
    b, s, _ = u.shape
    L = min(s, CHUNK)
    vr = vn.reshape(b, s // L, L, GM_GROUPS, GM_GROUP_DIM)
    w = jnp.tril(ws[:, :L, :L])
    mix = jnp.einsum('gts,bnsgc->bntgc', w, vr) + bs[:, :L].T[:, :, None]
    return u * mix.reshape(b, s, GM_WIDTH)


def merge(h, fox_o, gm_o, mem_o, w_gate, b_gate, w_br, w_o):
    gates = jax.nn.sigmoid(h @ w_gate + b_gate).reshape(h.shape[:-1] + (N_BRANCH, D_MODEL))
    br = jnp.stack([fox_o, gm_o, mem_o], axis=-2)
    proj = jnp.einsum('...ri,rid->...rd', br, w_br)
    return jnp.sum(gates * proj, axis=-2) @ w_o


def hier_moe(h, w_rg, b_rg, w_re, b_re, w_gu, w_dn):
    shp = h.shape
    hf = h.reshape(-1, D_MODEL)
    n = hf.shape[0]
    h32 = hf.astype(jnp.float32)
    g_prob = jax.nn.softmax(h32 @ w_rg.astype(jnp.float32) + b_rg.astype(jnp.float32), axis=-1)
    g_idx = jnp.argmax(g_prob, axis=-1)
    p_g = jnp.max(g_prob, axis=-1)
    e_logits = (h32 @ w_re.astype(jnp.float32) + b_re.astype(jnp.float32)).reshape(
        n, N_GROUPS, EXPERTS_PER_GROUP)
    e_sel = jnp.take_along_axis(e_logits, g_idx[:, None, None], axis=1)[:, 0]
    top_p, top_i = lax.top_k(jax.nn.softmax(e_sel, axis=-1), TOP_K)
    wts = top_p / jnp.sum(top_p, axis=-1, keepdims=True) * p_g[:, None]
    gid = g_idx[:, None] * EXPERTS_PER_GROUP + top_i
    comb = jnp.einsum('nk,nke->ne', wts, jax.nn.one_hot(gid, N_EXPERTS, dtype=jnp.float32)).astype(hf.dtype)
    y = jnp.zeros_like(hf)
    for e in range(N_EXPERTS):
        gate, up = jnp.split(hf @ w_gu[e], 2, axis=-1)
        y = y + comb[:, e:e + 1] * ((jax.nn.silu(gate) * up) @ w_dn[e])
    return y.reshape(shp)


def layer_forward(x, fox_fn, mk, mv, norm1_g, w_in, b_f, gm_norm_g, gm_ws, gm_bs,
                  w_gate, b_gate, w_br, w_o, norm2_g, w_rg, b_rg, w_re, b_re, w_gu, w_dn):
    h = rmsnorm(x, norm1_g)
    q, k, v, logf, u, gv, mq = project_in(h, w_in, b_f)
    fox_o = fox_fn(q, k, v, logf)
    u, vn = gmlp_front(u, gv, gm_norm_g)
    gm_o = spatial_gate(u, vn, gm_ws, gm_bs)
    mem_o = mem_attend(mq, mk, mv)
    x = x + merge(h, fox_o, gm_o, mem_o, w_gate, b_gate, w_br, w_o)
    x = x + hier_moe(rmsnorm(x, norm2_g), w_rg, b_rg, w_re, b_re, w_gu, w_dn)
    return x, (k, v, logf, vn)


def setup_inputs(seed: int = 0) -> dict:
    key = jax.random.key(seed)
    keys = iter(jax.random.split(key, 40))
    n_pages = PAST_LEN // PAGE_SIZE
    n_used = DEC_BATCH * n_pages
    n_pool = n_used + n_used // 4 + 1

    def nrm(shape, scale):
        return jax.random.normal(next(keys), shape, jnp.float32) * scale

    def gain(shape):
        return 1.0 + nrm(shape, 0.02)

    d = D_MODEL
    x_prompt = nrm((BATCH, SEQ, d), 1.0)
    x_sample = nrm((DEC_BATCH, DEC_SEQ, d), 1.0)
    cache_k = nrm((DEPTH, n_pool, PAGE_SIZE, FOX_HEADS, FOX_HEAD_DIM), 1.0)
    cache_v = nrm((DEPTH, n_pool, PAGE_SIZE, FOX_HEADS, FOX_HEAD_DIM), 1.0)
    cache_logf = jax.nn.log_sigmoid(jax.random.uniform(
        next(keys), (DEPTH, n_pool, PAGE_SIZE, FOX_HEADS), jnp.float32, 1.0, 6.0))
    cache_mem_k = nrm((DEPTH, DEC_BATCH, MEM_LEN, MEM_HEADS, MEM_HEAD_DIM), 1.0)
    cache_mem_v = nrm((DEPTH, DEC_BATCH, MEM_LEN, MEM_HEADS, MEM_HEAD_DIM), 1.0)
    page_table = jax.random.permutation(next(keys), n_pool)[:n_used].reshape(
        DEC_BATCH, n_pages).astype(jnp.int32)
    mem_prompt = nrm((BATCH, MEM_LEN, d), 1.0)
    return {
        'x_prompt': x_prompt,
        'x_sample': x_sample,
        'cache_k': cache_k,
        'cache_v': cache_v,
        'cache_logf': cache_logf,
        'cache_mem_k': cache_mem_k,
        'cache_mem_v': cache_mem_v,
        'page_table': page_table,
        'mem_prompt': mem_prompt,
        'norm1_g': gain((DEPTH, d)),
        'w_in': nrm((DEPTH, d, IN_WIDTH), d ** -0.5),
        'b_f': jax.random.uniform(next(keys), (DEPTH, FOX_HEADS), jnp.float32, 1.0, 6.0),
        'gm_norm_g': gain((DEPTH, GM_WIDTH)),
        'gm_ws': nrm((DEPTH, GM_GROUPS, CHUNK, CHUNK), CHUNK ** -0.5),
        'gm_bs': gain((DEPTH, GM_GROUPS, CHUNK)),
        'mem_norm_g': gain((DEPTH, d)),
        'w_mem_kv': nrm((DEPTH, d, 2 * MEM_WIDTH), d ** -0.5),
        'w_gate': nrm((DEPTH, d, N_BRANCH * d), d ** -0.5),
        'b_gate': nrm((DEPTH, N_BRANCH * d), 0.01),
        'w_br': nrm((DEPTH, N_BRANCH, BRANCH_WIDTH, d), BRANCH_WIDTH ** -0.5),
        'w_o': nrm((DEPTH, d, d), d ** -0.5),
        'norm2_g': gain((DEPTH, d)),
        'w_rg': nrm((DEPTH, d, N_GROUPS), d ** -0.5),
        'b_rg': nrm((DEPTH, N_GROUPS), 0.01),
        'w_re': nrm((DEPTH, d, N_EXPERTS), d ** -0.5),
        'b_re': nrm((DEPTH, N_EXPERTS), 0.01),
        'w_moe_gu': nrm((DEPTH, N_EXPERTS, d, 2 * EXPERT_FF), d ** -0.5),
        'w_moe_dn': nrm((DEPTH, N_EXPERTS, EXPERT_FF, d), EXPERT_FF ** -0.5),
        'final_g': gain((d,)),
    }


def reference(x_prompt, x_sample, cache_k, cache_v, cache_logf, cache_mem_k, cache_mem_v,
              page_table, mem_prompt, norm1_g, w_in, b_f, gm_norm_g, gm_ws, gm_bs, mem_norm_g,
              w_mem_kv, w_gate, b_gate, w_br, w_o, norm2_g, w_rg, b_rg, w_re, b_re,
              w_moe_gu, w_moe_dn, final_g):
    db = x_sample.shape[0]
    past_len = page_table.shape[1] * PAGE_SIZE
    yp, ys = x_prompt, x_sample
    kp_l, vp_l, fp_l, mkp_l, mvp_l = [], [], [], [], []
    ks_l, vs_l, fs_l, gs_l = [], [], [], []
    for l in range(DEPTH):
        lw = (norm1_g[l], w_in[l], b_f[l], gm_norm_g[l], gm_ws[l], gm_bs[l],
              w_gate[l], b_gate[l], w_br[l], w_o[l], norm2_g[l],
              w_rg[l], b_rg[l], w_re[l], b_re[l], w_moe_gu[l], w_moe_dn[l])
        mk_p, mv_p = mem_kv(mem_prompt, mem_norm_g[l], w_mem_kv[l])
        yp, (k_p, v_p, f_p, _) = layer_forward(yp, fox_prompt, mk_p, mv_p, *lw)
        k_past = cache_k[l][page_table].reshape(db, past_len, FOX_HEADS, FOX_HEAD_DIM)
        v_past = cache_v[l][page_table].reshape(db, past_len, FOX_HEADS, FOX_HEAD_DIM)
        f_past = cache_logf[l][page_table].reshape(db, past_len, FOX_HEADS)
        fox_s = functools.partial(fox_sample, k_past=k_past, v_past=v_past, logf_past=f_past)
        ys, (k_s, v_s, f_s, gv_s) = layer_forward(ys, fox_s, cache_mem_k[l], cache_mem_v[l], *lw)
        kp_l.append(k_p)
        vp_l.append(v_p)
        fp_l.append(f_p)
        mkp_l.append(mk_p)
        mvp_l.append(mv_p)
        ks_l.append(k_s)
        vs_l.append(v_s)
        fs_l.append(f_s)
        gs_l.append(gv_s)
    y_prompt = rmsnorm(yp, final_g)
    y_sample = rmsnorm(ys, final_g)
    return (y_prompt, y_sample, jnp.stack(kp_l), jnp.stack(vp_l), jnp.stack(fp_l),
            jnp.stack(mkp_l), jnp.stack(mvp_l), jnp.stack(ks_l), jnp.stack(vs_l),
            jnp.stack(fs_l), jnp.stack(gs_l))
```

```python
import functools

import numpy as np
import jax
import jax.numpy as jnp
from jax import lax
from jax.experimental import pallas as pl
from jax.experimental.pallas import tpu as pltpu

F32 = jnp.float32
BF16 = jnp.bfloat16
HIGHEST = lax.Precision.HIGHEST

EPS = 1e-6
FOX_HEADS = 8
FOX_HEAD_DIM = 64
FOX_WIDTH = FOX_HEADS * FOX_HEAD_DIM
GM_GROUPS = 4
GM_GROUP_DIM = 128
GM_WIDTH = GM_GROUPS * GM_GROUP_DIM
CHUNK = 128
MEM_HEADS = 4
MEM_HEAD_DIM = 128
MEM_WIDTH = MEM_HEADS * MEM_HEAD_DIM
N_BRANCH = 3
N_GROUPS = 4
EXPERTS_PER_GROUP = 8
N_EXPERTS = N_GROUPS * EXPERTS_PER_GROUP
EXPERT_FF = 512
PAGE_SIZE = 128

LANES = 128
VMEM_LIMIT_BYTES = 56 * 1024 * 1024
TOKEN_TILE = 256
ATTN_TQ = 1024
ATTN_TK = 512
CUMSUM_TILE = 512
MOE_TILE = 256
ROUTE_TILE = 512


def _params(*sem):
    return pltpu.CompilerParams(dimension_semantics=sem, vmem_limit_bytes=VMEM_LIMIT_BYTES)


def _rmsnorm(x, g):
    return x * lax.rsqrt(jnp.mean(x * x, axis=-1, keepdims=True) + EPS) * g


def _gelu(x):
    return 0.5 * x * (1.0 + jnp.tanh(np.sqrt(2.0 / np.pi).astype(np.float32) * (x + 0.044715 * (x * x * x))))


def _div(x, n):
    assert n & (n - 1) == 0
    return lax.shift_right_arithmetic(x, jnp.int32(n.bit_length() - 1))


def _mod(x, n):
    assert n & (n - 1) == 0
    return x & jnp.int32(n - 1)


def _const_spec(shape):
    zeros = (0,) * len(shape)
    return pl.BlockSpec(shape, lambda *_: zeros)


def _proj_in_kernel(x_ref, g_ref, w_ref, wf_ref, bf_ref, gmg_ref,
                    h_ref, q_ref, k_ref, v_ref, kb_ref, vb_ref, lf_ref, u_ref, vn_ref, mq_ref):
    hb = _rmsnorm(x_ref[...], g_ref[...]).astype(BF16)
    h_ref[...] = hb

    def proj(c):
        return jnp.dot(hb, w_ref[:, c * 512:(c + 1) * 512], preferred_element_type=F32)

    q_ref[...] = (proj(0) * (FOX_HEAD_DIM ** -0.5)).astype(BF16)
    k = proj(1)
    k_ref[...] = k
    kb_ref[...] = k.astype(BF16)
    v = proj(2)
    v_ref[...] = v
    vb_ref[...] = v.astype(BF16)
    u_ref[...] = _gelu(proj(3))
    vf = _gelu(proj(4))
    mu = jnp.mean(vf, axis=-1, keepdims=True)
    var = jnp.mean(jnp.square(vf - mu), axis=-1, keepdims=True)
    vn_ref[...] = (vf - mu) * lax.rsqrt(var + EPS) * gmg_ref[...]
    mq_ref[...] = proj(5)
    f = jnp.dot(hb, wf_ref[...], preferred_element_type=F32) + bf_ref[...]
    lf_ref[...] = jnp.minimum(f, 0.0) - jnp.log1p(jnp.exp(-jnp.abs(f)))


def _proj_in(x, g, w_main, w_f, b_f, gm_g):
    n, d = x.shape
    tm = TOKEN_TILE
    row = lambda width: pl.BlockSpec((tm, width), lambda i: (i, 0))
    out_shapes = (
        jax.ShapeDtypeStruct((n, d), BF16),
        jax.ShapeDtypeStruct((n, FOX_WIDTH), BF16),
        jax.ShapeDtypeStruct((n, FOX_WIDTH), F32),
        jax.ShapeDtypeStruct((n, FOX_WIDTH), F32),
        jax.ShapeDtypeStruct((n, FOX_WIDTH), BF16),
        jax.ShapeDtypeStruct((n, FOX_WIDTH), BF16),
        jax.ShapeDtypeStruct((n, LANES), F32),
        jax.ShapeDtypeStruct((n, GM_WIDTH), F32),
        jax.ShapeDtypeStruct((n, GM_WIDTH), F32),
        jax.ShapeDtypeStruct((n, MEM_WIDTH), F32),
    )
    return pl.pallas_call(
        _proj_in_kernel,
        grid=(n // tm,),
        in_specs=[row(d), _const_spec((1, d)), _const_spec(w_main.shape), _const_spec(w_f.shape),
                  _const_spec((1, LANES)), _const_spec((1, GM_WIDTH))],
        out_specs=(row(d), row(FOX_WIDTH), row(FOX_WIDTH), row(FOX_WIDTH), row(FOX_WIDTH),
                   row(FOX_WIDTH), row(LANES), row(GM_WIDTH), row(GM_WIDTH), row(MEM_WIDTH)),
        out_shape=out_shapes,
        compiler_params=_params("parallel"),
        name="proj_in",
    )(x, g, w_main, w_f, b_f, gm_g)


def _cumsum_kernel(lf_ref, o_ref, carry_ref):
    @pl.when(pl.program_id(1) == 0)
    def _():
        carry_ref[...] = jnp.zeros_like(carry_ref)

    tc = lf_ref.shape[-1]
    upper = (lax.broadcasted_iota(jnp.int32, (tc, tc), 0)
             <= lax.broadcasted_iota(jnp.int32, (tc, tc), 1)).astype(F32)
    y = jnp.dot(lf_ref[...], upper, precision=HIGHEST, preferred_element_type=F32) + carry_ref[...]
    o_ref[...] = y
    carry_ref[...] = y[:, tc - 1:tc]


def _cumsum_lanes(lf_t):
    b, h, s = lf_t.shape
    tc = min(CUMSUM_TILE, s)
    return pl.pallas_call(
        _cumsum_kernel,
        grid=(b, s // tc),
        in_specs=[pl.BlockSpec((None, h, tc), lambda i, j: (i, 0, j))],
        out_specs=pl.BlockSpec((None, h, tc), lambda i, j: (i, 0, j)),
        out_shape=jax.ShapeDtypeStruct((b, h, s), F32),
        scratch_shapes=[pltpu.VMEM((h, 1), F32)],
        compiler_params=_params("parallel", "arbitrary"),
        name="fox_cumsum",
    )(lf_t)


def _fox_attn_kernel(qi_tab, kj_tab, q_ref, k_ref, v_ref, fc_ref, fr_ref, o_ref,
                     qm_ref, m_ref, l_ref, acc_ref, *, tq, tk):
    p = pl.program_id(2)
    qi = qi_tab[p]
    kj = kj_tab[p]
    lane = lax.broadcasted_iota(jnp.int32, (tq, LANES), 1)
    low = lane < FOX_HEAD_DIM

    @pl.when(kj == 0)
    def _():
        q = q_ref[...]
        qm_ref[0] = jnp.where(low, q, jnp.zeros_like(q))
        qm_ref[1] = jnp.where(low, jnp.zeros_like(q), q)
        m_ref[...] = jnp.full_like(m_ref, -jnp.inf)
        l_ref[...] = jnp.zeros_like(l_ref)
        acc_ref[...] = jnp.zeros_like(acc_ref)

    k2 = k_ref[...]
    v2 = v_ref[...]
    row = qi * tq + lax.broadcasted_iota(jnp.int32, (tq, tk), 0)
    col = kj * tk + lax.broadcasted_iota(jnp.int32, (tq, tk), 1)
    causal = col <= row
    alphas, pvs = [], []
    for hh in range(2):
        s = lax.dot_general(qm_ref[hh], k2, (((1,), (1,)), ((), ())), preferred_element_type=F32)
        t = jnp.where(causal, s - fc_ref[hh:hh + 1, :], -jnp.inf)
        f_row = fr_ref[:, hh:hh + 1]
        m_prev = m_ref[hh]
        m_new = jnp.maximum(m_prev, jnp.max(t, axis=1, keepdims=True) + f_row)
        alpha = jnp.exp(m_prev - m_new)
        pr = jnp.exp(t - (m_new - f_row))
        l_ref[hh] = alpha * l_ref[hh] + jnp.sum(pr, axis=1, keepdims=True)
        m_ref[hh] = m_new
        alphas.append(alpha)
        pvs.append(jnp.dot(pr.astype(BF16), v2, preferred_element_type=F32))
    acc = acc_ref[...]
    acc_ref[...] = jnp.where(low, alphas[0] * acc + pvs[0], alphas[1] * acc + pvs[1])

    @pl.when((kj + 1) * tk >= (qi + 1) * tq)
    def _():
        inv = jnp.where(low, 1.0 / l_ref[0], 1.0 / l_ref[1])
        o_ref[...] = (acc_ref[...] * inv).astype(BF16)


def _fox_prompt(q, kb, vb, f_col, f_row, batch, seq):
    tq = min(ATTN_TQ, seq)
    tk = min(ATTN_TK, tq)
    nq = seq // tq
    pairs = [(i, j) for i in range(nq) for j in range(((i + 1) * tq) // tk)]
    qi_tab = jnp.asarray(np.array([a for a, _ in pairs], np.int32))
    kj_tab = jnp.asarray(np.array([c for _, c in pairs], np.int32))
    nqb, nkb = seq // tq, seq // tk
    grid_spec = pltpu.PrefetchScalarGridSpec(
        num_scalar_prefetch=2,
        grid=(batch, FOX_HEADS // 2, len(pairs)),
        in_specs=[
            pl.BlockSpec((tq, LANES), lambda b, h, p, qt, kt: (b * nqb + qt[p], h)),
            pl.BlockSpec((tk, LANES), lambda b, h, p, qt, kt: (b * nkb + kt[p], h)),
            pl.BlockSpec((tk, LANES), lambda b, h, p, qt, kt: (b * nkb + kt[p], h)),
            pl.BlockSpec((None, None, 2, tk), lambda b, h, p, qt, kt: (b, h, 0, kt[p])),
            pl.BlockSpec((None, None, tq, 2), lambda b, h, p, qt, kt: (b, h, qt[p], 0)),
        ],
        out_specs=pl.BlockSpec((tq, LANES), lambda b, h, p, qt, kt: (b * nqb + qt[p], h)),
        scratch_shapes=[pltpu.VMEM((2, tq, LANES), BF16), pltpu.VMEM((2, tq, 1), F32),
                        pltpu.VMEM((2, tq, 1), F32), pltpu.VMEM((tq, LANES), F32)],
    )
    return pl.pallas_call(
        functools.partial(_fox_attn_kernel, tq=tq, tk=tk),
        grid_spec=grid_spec,
        out_shape=jax.ShapeDtypeStruct((batch * seq, FOX_WIDTH), BF16),
        compiler_params=_params("parallel", "parallel", "arbitrary"),
        name="fox_prompt_attn",
    )(qi_tab, kj_tab, q, kb, vb, f_col, f_row)


def _fox_sample_kernel(pt_ref, q_ref, kn_ref, vn_ref, lfn_ref, *refs, n_pages, t_new):
    k_refs = refs[:n_pages]
    v_refs = refs[n_pages:2 * n_pages]
    lf_refs = refs[2 * n_pages:3 * n_pages]
    o_ref = refs[3 * n_pages]
    t_ref = refs[3 * n_pages + 1]
    rows = FOX_HEADS * t_new

    q = q_ref[...]
    qt = jnp.concatenate([q] * FOX_HEADS, axis=0)
    r_w = lax.broadcasted_iota(jnp.int32, (rows, FOX_WIDTH), 0)
    c_w = lax.broadcasted_iota(jnp.int32, (rows, FOX_WIDTH), 1)
    head_match = _div(r_w, t_new) == _div(c_w, FOX_HEAD_DIM)
    q_bd = jnp.where(head_match, qt, 0.0).astype(BF16)
    upper = (lax.broadcasted_iota(jnp.int32, (PAGE_SIZE, PAGE_SIZE), 0)
             <= lax.broadcasted_iota(jnp.int32, (PAGE_SIZE, PAGE_SIZE), 1)).astype(F32)
    r_p = lax.broadcasted_iota(jnp.int32, (rows, PAGE_SIZE), 0)
    c_p = lax.broadcasted_iota(jnp.int32, (rows, PAGE_SIZE), 1)
    pad = jnp.zeros((PAGE_SIZE - t_new, FOX_WIDTH), F32)

    carry = jnp.zeros((FOX_HEADS, 1), F32)
    m = jnp.full((rows, 1), -jnp.inf, F32)
    f_row = None
    for pg in range(n_pages + 1):
        if pg < n_pages:
            kp = k_refs[pg][...]
            lf_t = lf_refs[pg][...]
        else:
            kp = jnp.concatenate([kn_ref[...], pad], axis=0)
            lf_t = lfn_ref[...]
        s = lax.dot_general(q_bd, kp.astype(BF16), (((1,), (1,)), ((), ())), preferred_element_type=F32)
        f_t = jnp.dot(lf_t, upper, precision=HIGHEST, preferred_element_type=F32) + carry
        carry = f_t[:, PAGE_SIZE - 1:PAGE_SIZE]
        f_col = jnp.concatenate(
            [jnp.broadcast_to(f_t[h:h + 1, :], (t_new, PAGE_SIZE)) for h in range(FOX_HEADS)], axis=0)
        t = s - f_col
        if pg == n_pages:
            t = jnp.where(c_p <= _mod(r_p, t_new), t, -jnp.inf)
            f_row = jnp.sum(jnp.where(c_p == _mod(r_p, t_new), f_col, 0.0), axis=1, keepdims=True)
        t_ref[pg] = t
        m = jnp.maximum(m, jnp.max(t, axis=1, keepdims=True))
    m_logit = m + f_row
    shift = m_logit - f_row
    l = jnp.zeros((rows, 1), F32)
    acc = jnp.zeros((rows, FOX_WIDTH), F32)
    for pg in range(n_pages + 1):
        pr = jnp.exp(t_ref[pg] - shift)
        l = l + jnp.sum(pr, axis=1, keepdims=True)
        vp = v_refs[pg][...] if pg < n_pages else jnp.concatenate([vn_ref[...], pad], axis=0)
        acc = acc + jnp.dot(pr.astype(BF16), vp.astype(BF16), preferred_element_type=F32)
    o = acc / l
    c_o = lax.broadcasted_iota(jnp.int32, (t_new, FOX_WIDTH), 1)
    out = jnp.zeros((t_new, FOX_WIDTH), F32)
    for h in range(FOX_HEADS):
        out = jnp.where(_div(c_o, FOX_HEAD_DIM) == h, o[h * t_new:(h + 1) * t_new, :], out)
    o_ref[...] = out


def _fox_sample(page_table, q32, k32, v32, lf_new_t, cache_k, cache_v, cache_lf_t, layer, row0, t_new):
    nb, n_pages = page_table.shape
    blk0 = row0 // t_new
    tok = pl.BlockSpec((t_new, FOX_WIDTH), lambda b, pt: (blk0 + b, 0))

    def page_spec(shape, pg):
        return pl.BlockSpec((None, None) + shape, lambda b, pt: (layer, pt[b, pg], 0, 0))

    in_specs = [tok, tok, tok, pl.BlockSpec((None, FOX_HEADS, LANES), lambda b, pt: (b, 0, 0))]
    in_specs += [page_spec((PAGE_SIZE, FOX_WIDTH), pg) for pg in range(n_pages)]
    in_specs += [page_spec((PAGE_SIZE, FOX_WIDTH), pg) for pg in range(n_pages)]
    in_specs += [page_spec((FOX_HEADS, PAGE_SIZE), pg) for pg in range(n_pages)]
    grid_spec = pltpu.PrefetchScalarGridSpec(
        num_scalar_prefetch=1,
        grid=(nb,),
        in_specs=in_specs,
        out_specs=pl.BlockSpec((t_new, FOX_WIDTH), lambda b, pt: (b, 0)),
        scratch_shapes=[pltpu.VMEM((n_pages + 1, FOX_HEADS * t_new, PAGE_SIZE), F32)],
    )
    return pl.pallas_call(
        functools.partial(_fox_sample_kernel, n_pages=n_pages, t_new=t_new),
        grid_spec=grid_spec,
        out_shape=jax.ShapeDtypeStruct((nb * t_new, FOX_WIDTH), F32),
        compiler_params=_params("parallel"),
        name="fox_sample_attn",
    )(page_table, q32, k32, v32, lf_new_t, *([cache_k] * n_pages), *([cache_v] * n_pages),
      *([cache_lf_t] * n_pages))


def _norm_matmul_kernel(x_ref, g_ref, w_ref, o_ref):
    hb = _rmsnorm(x_ref[...], g_ref[...]).astype(BF16)
    o_ref[...] = jnp.dot(hb, w_ref[...], preferred_element_type=F32)


def _norm_matmul(x, g, w):
    n, d = x.shape
    tm = min(TOKEN_TILE, n)
    return pl.pallas_call(
        _norm_matmul_kernel,
        grid=(n // tm,),
        in_specs=[pl.BlockSpec((tm, d), lambda i: (i, 0)), _const_spec((1, d)), _const_spec(w.shape)],
        out_specs=pl.BlockSpec((tm, w.shape[1]), lambda i: (i, 0)),
        out_shape=jax.ShapeDtypeStruct((n, w.shape[1]), F32),
        compiler_params=_params("parallel"),
        name="mem_kv",
    )(x, g, w)


def _mem_attend_kernel(q_ref, mk_ref, mv_ref, o_ref):
    scale = MEM_HEAD_DIM ** -0.5
    for h in range(MEM_HEADS):
        sl = slice(h * MEM_HEAD_DIM, (h + 1) * MEM_HEAD_DIM)
        s = lax.dot_general(q_ref[:, sl].astype(BF16), mk_ref[:, sl].astype(BF16),
                            (((1,), (1,)), ((), ())), preferred_element_type=F32) * scale
        e = jnp.exp(s - jnp.max(s, axis=1, keepdims=True))
        pr = e / jnp.sum(e, axis=1, keepdims=True)
        o_ref[:, sl] = jnp.dot(pr.astype(BF16), mv_ref[:, sl].astype(BF16), preferred_element_type=F32)


def _mem_attend(mq, mk, mv, row0, n_mem, t_per_mem, name):
    tm = min(TOKEN_TILE * 2, t_per_mem)
    tiles = t_per_mem // tm
    blk0 = row0 // tm
    m_len = mk.shape[1]
    mem_spec = pl.BlockSpec((None, m_len, MEM_WIDTH), lambda b, i: (b, 0, 0))
    return pl.pallas_call(
        _mem_attend_kernel,
        grid=(n_mem, tiles),
        in_specs=[pl.BlockSpec((tm, MEM_WIDTH), lambda b, i: (blk0 + b * tiles + i, 0)), mem_spec, mem_spec],
        out_specs=pl.BlockSpec((tm, MEM_WIDTH), lambda b, i: (b * tiles + i, 0)),
        out_shape=jax.ShapeDtypeStruct((n_mem * t_per_mem, MEM_WIDTH), F32),
        compiler_params=_params("parallel", "parallel"),
        name=name,
    )(mq, mk, mv)


def _merge_kernel(x_ref, h_ref, fox_ref, u_ref, vn_ref, mem_ref, ws_ref, bs_ref,
                  wg_ref, bg_ref, wbr_ref, wo_ref, o_ref, gm_ref, *, chunk):
    tm, d = x_ref.shape
    r_c = lax.broadcasted_iota(jnp.int32, (CHUNK, CHUNK), 0)
    c_c = lax.broadcasted_iota(jnp.int32, (CHUNK, CHUNK), 1)
    keep = (c_c <= r_c) & (_div(r_c, chunk) == _div(c_c, chunk))
    for g in range(GM_GROUPS):
        w = jnp.where(keep, ws_ref[g], 0.0).astype(BF16)
        sl = slice(g * GM_GROUP_DIM, (g + 1) * GM_GROUP_DIM)
        for c in range(tm // CHUNK):
            rs = slice(c * CHUNK, (c + 1) * CHUNK)
            mix = jnp.dot(w, vn_ref[rs, sl].astype(BF16), preferred_element_type=F32) + bs_ref[:, g:g + 1]
            gm_ref[rs, sl] = (u_ref[rs, sl] * mix).astype(BF16)

    hb = h_ref[...]
    branches = (fox_ref[...], gm_ref[...], mem_ref[...].astype(BF16))
    merged = None
    for r in range(N_BRANCH):
        gate = jax.nn.sigmoid(jnp.dot(hb, wg_ref[:, r * d:(r + 1) * d], preferred_element_type=F32)
                              + bg_ref[:, r * d:(r + 1) * d])
        term = gate * jnp.dot(branches[r], wbr_ref[r], preferred_element_type=F32)
        merged = term if merged is None else merged + term
    o_ref[...] = x_ref[...] + jnp.dot(merged.astype(BF16), wo_ref[...], preferred_element_type=F32)


def _merge(x, h, fox_o, u, vn, mem_o, ws, bs_t, w_gate, b_gate, w_br, w_o, tile0, ntiles, chunk, name):
    d = x.shape[1]
    tm = TOKEN_TILE
    row = lambda width: pl.BlockSpec((tm, width), lambda i: (tile0 + i, 0))
    in_specs = [row(d), row(d), row(FOX_WIDTH), row(GM_WIDTH), row(GM_WIDTH), row(MEM_WIDTH),
                _const_spec(ws.shape), _const_spec(bs_t.shape),
                _const_spec(w_gate.shape), _const_spec((1, N_BRANCH * d)), _const_spec(w_br.shape),
                _const_spec(w_o.shape)]
    return pl.pallas_call(
        functools.partial(_merge_kernel, chunk=chunk),
        grid=(ntiles,),
        in_specs=in_specs,
        out_specs=pl.BlockSpec((tm, d), lambda i: (i, 0)),
        out_shape=jax.ShapeDtypeStruct((ntiles * tm, d), F32),
        scratch_shapes=[pltpu.VMEM((tm, GM_WIDTH), BF16)],
        compiler_params=_params("parallel"),
        name=name,
    )(x, h, fox_o, u, vn, mem_o, ws, bs_t, w_gate, b_gate, w_br, w_o)


def _moe_route_kernel(x_ref, g_ref, wr_ref, br_ref, hn_ref, meta_ref, cnt_ref, carry_ref):
    @pl.when(pl.program_id(0) == 0)
    def _():
        carry_ref[...] = jnp.zeros_like(carry_ref)

    tm = x_ref.shape[0]
    hn = _rmsnorm(x_ref[...], g_ref[...])
    hn_ref[...] = hn
    logits = jnp.dot(hn, wr_ref[...], precision=HIGHEST, preferred_element_type=F32) + br_ref[...]
    col = lax.broadcasted_iota(jnp.int32, (tm, LANES), 1)
    colf = col.astype(F32)
    big = jnp.float32(1e9)

    def masked_softmax(mask):
        z = jnp.where(mask, logits, -jnp.inf)
        e = jnp.exp(z - jnp.max(z, axis=1, keepdims=True))
        return e / jnp.sum(e, axis=1, keepdims=True)

    def first_argmax(vals, vmax, mask):
        return jnp.min(jnp.where(mask & (vals == vmax), colf, big), axis=1, keepdims=True)

    is_group = col < N_GROUPS
    g_prob = masked_softmax(is_group)
    p_g = jnp.max(g_prob, axis=1, keepdims=True)
    g_idx = first_argmax(g_prob, p_g, is_group)
    e_col = col - N_GROUPS
    in_group = (e_col >= 0) & (e_col < N_EXPERTS) & (_div(e_col, EXPERTS_PER_GROUP).astype(F32) == g_idx)
    e_prob = masked_softmax(in_group)
    p1 = jnp.max(e_prob, axis=1, keepdims=True)
    i1 = first_argmax(e_prob, p1, in_group)
    rest = in_group & (colf != i1)
    p2 = jnp.max(jnp.where(rest, e_prob, -1.0), axis=1, keepdims=True)
    i2 = first_argmax(e_prob, p2, rest)
    denom = p1 + p2
    w1 = p1 / denom * p_g
    w2 = p2 / denom * p_g
    e1 = i1 - N_GROUPS
    e2 = i2 - N_GROUPS

    onehot = ((colf == e1) | (colf == e2)).astype(F32)
    lower = (lax.broadcasted_iota(jnp.int32, (tm, tm), 1)
             < lax.broadcasted_iota(jnp.int32, (tm, tm), 0)).astype(BF16)
    before = jnp.dot(lower, onehot.astype(BF16), preferred_element_type=F32) + carry_ref[...]
    r1 = jnp.sum(jnp.where(colf == e1, before, 0.0), axis=1, keepdims=True)
    r2 = jnp.sum(jnp.where(colf == e2, before, 0.0), axis=1, keepdims=True)
    carry_ref[...] = carry_ref[...] + jnp.sum(onehot, axis=0, keepdims=True)
    cnt_ref[...] = jnp.broadcast_to(carry_ref[...], cnt_ref.shape)

    meta = jnp.zeros((tm, LANES), F32)
    for lane_id, val in enumerate((e1, e2, r1, r2, w1, w2)):
        meta = jnp.where(col == lane_id, val, meta)
    meta_ref[...] = meta


def _moe_route(x, g, w_route, b_route):
    n, d = x.shape
    tm = ROUTE_TILE
    return pl.pallas_call(
        _moe_route_kernel,
        grid=(n // tm,),
        in_specs=[pl.BlockSpec((tm, d), lambda i: (i, 0)), _const_spec((1, d)),
                  _const_spec(w_route.shape), _const_spec((1, LANES))],
        out_specs=(pl.BlockSpec((tm, d), lambda i: (i, 0)), pl.BlockSpec((tm, LANES), lambda i: (i, 0)),
                   _const_spec((8, LANES))),
        out_shape=(jax.ShapeDtypeStruct((n, d), F32), jax.ShapeDtypeStruct((n, LANES), F32),
                   jax.ShapeDtypeStruct((8, LANES), F32)),
        scratch_shapes=[pltpu.VMEM((1, LANES), F32)],
        compiler_params=_params("arbitrary"),
        name="moe_route",
    )(x, g, w_route, b_route)


def _row_copy(src_hbm, row, dst, dst_row, sem):
    return pltpu.make_async_copy(src_hbm.at[pl.ds(row, 1)], dst.at[pl.ds(dst_row, 1)], sem.at[0])


def _moe_expert_kernel(te_ref, tv_ref, src_ref, hn_hbm, wgu_ref, wdn_ref, ys_ref,
                       x_buf, wgu_bf, wdn_bf, sem, *, tg):
    t = pl.program_id(0)
    valid = tv_ref[t] == 1

    @pl.when(valid)
    def _():
        def issue(r, c):
            _row_copy(hn_hbm, src_ref[0, r], x_buf, r, sem).start()
            return c

        lax.fori_loop(0, tg, issue, 0)

        @pl.when((t == 0) | (te_ref[t] != te_ref[jnp.maximum(t - 1, 0)]))
        def _():
            wgu_bf[...] = wgu_ref[...].astype(BF16)
            wdn_bf[...] = wdn_ref[...].astype(BF16)

        def wait(r, c):
            _row_copy(hn_hbm, 0, x_buf, r, sem).wait()
            return c

        lax.fori_loop(0, tg, wait, 0)
        gu = jnp.dot(x_buf[...].astype(BF16), wgu_bf[...], preferred_element_type=F32)
        gate = gu[:, :EXPERT_FF]
        act = gate * jax.nn.sigmoid(gate) * gu[:, EXPERT_FF:]
        ys_ref[...] = jnp.dot(act.astype(BF16), wdn_bf[...], preferred_element_type=F32)

    @pl.when(jnp.logical_not(valid))
    def _():
        ys_ref[...] = jnp.zeros_like(ys_ref)


def _moe_experts(tile_expert, tile_valid, src_idx, hn, w_gu, w_dn, layer):
    n_tiles = tile_expert.shape[0]
    tg = MOE_TILE
    d = hn.shape[1]
    grid_spec = pltpu.PrefetchScalarGridSpec(
        num_scalar_prefetch=2,
        grid=(n_tiles,),
        in_specs=[
            pl.BlockSpec((None, 1, tg), lambda t, te, tv: (t, 0, 0), memory_space=pltpu.SMEM),
            pl.BlockSpec(memory_space=pl.ANY),
            pl.BlockSpec((None, None, d, 2 * EXPERT_FF), lambda t, te, tv: (layer, te[t], 0, 0)),
            pl.BlockSpec((None, None, EXPERT_FF, d), lambda t, te, tv: (layer, te[t], 0, 0)),
        ],
        out_specs=pl.BlockSpec((tg, d), lambda t, te, tv: (t, 0)),
        scratch_shapes=[pltpu.VMEM((tg, d), F32), pltpu.VMEM((d, 2 * EXPERT_FF), BF16),
                        pltpu.VMEM((EXPERT_FF, d), BF16), pltpu.SemaphoreType.DMA((1,))],
    )
    return pl.pallas_call(
        functools.partial(_moe_expert_kernel, tg=tg),
        grid_spec=grid_spec,
        out_shape=jax.ShapeDtypeStruct((n_tiles * tg, d), F32),
        compiler_params=_params("arbitrary"),
        name="moe_experts",
    )(tile_expert, tile_valid, src_idx, hn, w_gu, w_dn)


def _moe_combine_kernel(dest_ref, x_ref, meta_ref, fg_ref, ys_hbm, o_ref, buf, sem, *, tm, final):
    def issue(r, c):
        _row_copy(ys_hbm, dest_ref[0, 2 * r], buf.at[0], r, sem).start()
        _row_copy(ys_hbm, dest_ref[0, 2 * r + 1], buf.at[1], r, sem).start()
        return c

    lax.fori_loop(0, tm, issue, 0)

    def wait(r, c):
        _row_copy(ys_hbm, 0, buf.at[0], r, sem).wait()
        _row_copy(ys_hbm, 0, buf.at[1], r, sem).wait()
        return c

    lax.fori_loop(0, tm, wait, 0)
    meta = meta_ref[...]
    y = x_ref[...] + (meta[:, 4:5] * buf[0] + meta[:, 5:6] * buf[1])
    if final:
        y = _rmsnorm(y, fg_ref[...])
    o_ref[...] = y


def _moe_combine(dest, x, meta, final_g, ys, final):
    n, d = x.shape
    tm = TOKEN_TILE
    return pl.pallas_call(
        functools.partial(_moe_combine_kernel, tm=tm, final=final),
        grid=(n // tm,),
        in_specs=[pl.BlockSpec((None, 1, 2 * tm), lambda i: (i, 0, 0), memory_space=pltpu.SMEM),
                  pl.BlockSpec((tm, d), lambda i: (i, 0)), pl.BlockSpec((tm, LANES), lambda i: (i, 0)),
                  _const_spec((1, d)), pl.BlockSpec(memory_space=pl.ANY)],
        out_specs=pl.BlockSpec((tm, d), lambda i: (i, 0)),
        out_shape=jax.ShapeDtypeStruct((n, d), F32),
        scratch_shapes=[pltpu.VMEM((2, tm, d), F32), pltpu.SemaphoreType.DMA((1,))],
        compiler_params=_params("arbitrary"),
        name="moe_combine",
    )(dest, x, meta, final_g, ys)


def _moe_layer(x, g, w_route, b_route, w_gu, w_dn, layer, final_g, final):
    n, d = x.shape
    tg = MOE_TILE
    hn, meta, counts = _moe_route(x, g, w_route, b_route)
    counts = counts[0, :N_EXPERTS].astype(jnp.int32)
    padded = ((counts + tg - 1) // tg) * tg
    ends = jnp.cumsum(padded)
    starts = ends - padded
    eid = meta[:, 0:2].astype(jnp.int32)
    rank = meta[:, 2:4].astype(jnp.int32)
    dest = starts[eid] + rank
    n_tiles = (2 * n) // tg + N_EXPERTS
    tile_start = jnp.arange(n_tiles, dtype=jnp.int32) * tg
    tile_expert = jnp.minimum(jnp.searchsorted(ends, tile_start, side="right"), N_EXPERTS - 1).astype(jnp.int32)
    tile_valid = (tile_start < ends[-1]).astype(jnp.int32)
    token = jnp.broadcast_to(jnp.arange(n, dtype=jnp.int32)[:, None], (n, 2))
    src_idx = jnp.zeros((n_tiles * tg,), jnp.int32).at[dest.reshape(-1)].set(token.reshape(-1))
    ys = _moe_experts(tile_expert, tile_valid, src_idx.reshape(n_tiles, 1, tg), hn, w_gu, w_dn, layer)
    return _moe_combine(dest.reshape(n // TOKEN_TILE, 1, 2 * TOKEN_TILE), x, meta, final_g, ys, final)


def kernel(x_prompt, x_sample, cache_k, cache_v, cache_logf, cache_mem_k, cache_mem_v, page_table, mem_prompt, norm1_g, w_in, b_f, gm_norm_g, gm_ws, gm_bs, mem_norm_g, w_mem_kv, w_gate, b_gate, w_br, w_o, norm2_g, w_rg, b_rg, w_re, b_re, w_moe_gu, w_moe_dn, final_g):
    batch, seq, d = x_prompt.shape
    dec_batch, dec_seq, _ = x_sample.shape
    depth = w_in.shape[0]
    n_pool = cache_k.shape[1]
    mem_len = mem_prompt.shape[1]
    n_p = batch * seq
    n_s = dec_batch * dec_seq
    assert n_p % (2 * TOKEN_TILE) == 0 and n_s % (2 * TOKEN_TILE) == 0 and seq % CHUNK == 0
    assert CHUNK % dec_seq == 0 and (n_p + n_s) % ROUTE_TILE == 0
    n_prompt_tiles = n_p // TOKEN_TILE

    x = jnp.concatenate([x_prompt.reshape(n_p, d), x_sample.reshape(n_s, d)], axis=0)
    mem_flat = mem_prompt.reshape(batch * mem_len, d)
    ck = cache_k.reshape(depth, n_pool, PAGE_SIZE, FOX_WIDTH)
    cv = cache_v.reshape(depth, n_pool, PAGE_SIZE, FOX_WIDTH)
    clf_t = jnp.swapaxes(cache_logf, 2, 3)

    q_end, f_end = 3 * FOX_WIDTH, 3 * FOX_WIDTH + FOX_HEADS
    reps = CHUNK // dec_seq
    outs = {name: [] for name in ("kp", "vp", "fp", "mkp", "mvp", "ks", "vs", "fs", "gs")}
    for l in range(depth):
        w_main = jnp.concatenate([w_in[l][:, :q_end], w_in[l][:, f_end:]], axis=1).astype(BF16)
        w_f = jnp.pad(w_in[l][:, q_end:f_end], ((0, 0), (0, LANES - FOX_HEADS))).astype(BF16)
        bf_pad = jnp.pad(b_f[l], (0, LANES - FOX_HEADS)).reshape(1, LANES)
        ws_sample = jnp.tile(gm_ws[l][:, :dec_seq, :dec_seq], (1, reps, reps))
        bs_sample_t = jnp.tile(gm_bs[l][:, :dec_seq], (1, reps)).T
        w_route = jnp.pad(jnp.concatenate([w_rg[l], w_re[l]], axis=1),
                          ((0, 0), (0, LANES - N_GROUPS - N_EXPERTS)))
        b_route = jnp.pad(jnp.concatenate([b_rg[l], b_re[l]]), (0, LANES - N_GROUPS - N_EXPERTS)).reshape(1, LANES)

        h, q, k32, v32, kb, vb, lf, u, vn, mq = _proj_in(
            x, norm1_g[l].reshape(1, d), w_main, w_f, bf_pad, gm_norm_g[l].reshape(1, GM_WIDTH))

        lf_p = lf[:n_p, :FOX_HEADS].reshape(batch, seq, FOX_HEADS)
        f_t = _cumsum_lanes(jnp.swapaxes(lf_p, 1, 2))
        f_col = f_t.reshape(batch, FOX_HEADS // 2, 2, seq)
        f_row = jnp.swapaxes(f_col, 2, 3)
        fox_p = _fox_prompt(q, kb, vb, f_col, f_row, batch, seq)

        lf_s = lf[n_p:, :FOX_HEADS].reshape(dec_batch, dec_seq, FOX_HEADS)
        lf_new_t = jnp.pad(jnp.swapaxes(lf_s, 1, 2), ((0, 0), (0, 0), (0, LANES - dec_seq)))
        fox_s = _fox_sample(page_table, q.astype(F32), k32, v32, lf_new_t, ck, cv, clf_t, l, n_p, dec_seq)
        fox_o = jnp.concatenate([fox_p, fox_s.astype(BF16)], axis=0)

        kv = _norm_matmul(mem_flat, mem_norm_g[l].reshape(1, d), w_mem_kv[l].astype(BF16))
        mk_p = kv[:, :MEM_WIDTH].reshape(batch, mem_len, MEM_WIDTH)
        mv_p = kv[:, MEM_WIDTH:].reshape(batch, mem_len, MEM_WIDTH)
        mem_p = _mem_attend(mq, mk_p, mv_p, 0, batch, seq, "mem_attend_prompt")
        mem_s = _mem_attend(mq, cache_mem_k[l].reshape(dec_batch, mem_len, MEM_WIDTH),
                            cache_mem_v[l].reshape(dec_batch, mem_len, MEM_WIDTH), n_p, dec_batch, dec_seq,
                            "mem_attend_sample")
        mem_o = jnp.concatenate([mem_p, mem_s], axis=0)

        merge_w = (w_gate[l].astype(BF16), b_gate[l].reshape(1, N_BRANCH * d), w_br[l].astype(BF16),
                   w_o[l].astype(BF16))
        x_p = _merge(x, h, fox_o, u, vn, mem_o, gm_ws[l], gm_bs[l].T, *merge_w,
                     0, n_prompt_tiles, CHUNK, "merge_prompt")
        x_s = _merge(x, h, fox_o, u, vn, mem_o, ws_sample, bs_sample_t, *merge_w,
                     n_prompt_tiles, n_s // TOKEN_TILE, dec_seq, "merge_sample")
        x = jnp.concatenate([x_p, x_s], axis=0)
        x = _moe_layer(x, norm2_g[l].reshape(1, d), w_route, b_route, w_moe_gu, w_moe_dn, l,
                       final_g.reshape(1, d), l == depth - 1)

        outs["kp"].append(k32[:n_p].reshape(batch, seq, FOX_HEADS, FOX_HEAD_DIM))
        outs["vp"].append(v32[:n_p].reshape(batch, seq, FOX_HEADS, FOX_HEAD_DIM))
        outs["fp"].append(lf_p)
        outs["mkp"].append(mk_p.reshape(batch, mem_len, MEM_HEADS, MEM_HEAD_DIM))
        outs["mvp"].append(mv_p.reshape(batch, mem_len, MEM_HEADS, MEM_HEAD_DIM))
        outs["ks"].append(k32[n_p:].reshape(dec_batch, dec_seq, FOX_HEADS, FOX_HEAD_DIM))
        outs["vs"].append(v32[n_p:].reshape(dec_batch, dec_seq, FOX_HEADS, FOX_HEAD_DIM))
        outs["fs"].append(lf_s)
        outs["gs"].append(vn[n_p:].reshape(dec_batch, dec_seq, GM_WIDTH))

    y_prompt = x[:n_p].reshape(batch, seq, d)
    y_sample = x[n_p:].reshape(dec_batch, dec_seq, d)
    return (y_prompt, y_sample) + tuple(
        jnp.stack(outs[name]) for name in ("kp", "vp", "fp", "mkp", "mvp", "ks", "vs", "fs", "gs"))
```

```python
import functools

import numpy as np
import jax
import jax.numpy as jnp
from jax import lax
from jax.experimental import pallas as pl
from jax.experimental.pallas import tpu as pltpu

F32 = jnp.float32
BF16 = jnp.bfloat16
HIGHEST = lax.Precision.HIGHEST

EPS = 1e-6
LOG2E = 1.4426950408889634
FOX_HEADS = 8
FOX_HEAD_DIM = 64
FOX_WIDTH = FOX_HEADS * FOX_HEAD_DIM
GM_GROUPS = 4
GM_GROUP_DIM = 128
GM_WIDTH = GM_GROUPS * GM_GROUP_DIM
CHUNK = 128
MEM_HEADS = 4
MEM_HEAD_DIM = 128
MEM_WIDTH = MEM_HEADS * MEM_HEAD_DIM
N_BRANCH = 3
N_GROUPS = 4
EXPERTS_PER_GROUP = 8
N_EXPERTS = N_GROUPS * EXPERTS_PER_GROUP
EXPERT_FF = 512
PAGE_SIZE = 128

LANES = 128
VMEM_LIMIT_BYTES = 56 * 1024 * 1024
TOKEN_TILE = 256
ATTN_TQ = 1024
ATTN_TK = 512
ATTN_ROWS = 16
ATTN_MAX_UNROLL = 64
ATTN_EXP_UNROLL = 64
CUMSUM_TILE = 512
MOE_TILE = 256
ROUTE_TILE = 512
DMA_UNROLL = 8


def _params(*sem):
    return pltpu.CompilerParams(dimension_semantics=sem, vmem_limit_bytes=VMEM_LIMIT_BYTES)


def _rmsnorm(x, g):
    return x * lax.rsqrt(jnp.mean(x * x, axis=-1, keepdims=True) + EPS) * g


def _gelu(x):
    return 0.5 * x * (1.0 + jnp.tanh(np.sqrt(2.0 / np.pi).astype(np.float32) * (x + 0.044715 * (x * x * x))))


def _div(x, n):
    assert n & (n - 1) == 0
    return lax.shift_right_arithmetic(x, jnp.int32(n.bit_length() - 1))


def _mod(x, n):
    assert n & (n - 1) == 0
    return x & jnp.int32(n - 1)


def _const_spec(shape):
    zeros = (0,) * len(shape)
    return pl.BlockSpec(shape, lambda *_: zeros)


def _proj_in_kernel(x_ref, g_ref, w_ref, wf_ref, bf_ref, gmg_ref,
                    h_ref, q_ref, k_ref, v_ref, kb_ref, vb_ref, lf_ref, u_ref, vn_ref, mq_ref):
    hb = _rmsnorm(x_ref[...], g_ref[...]).astype(BF16)
    h_ref[...] = hb

    def proj(c):
        return jnp.dot(hb, w_ref[:, c * 512:(c + 1) * 512], preferred_element_type=F32)

    q_ref[...] = (proj(0) * (FOX_HEAD_DIM ** -0.5 * LOG2E)).astype(BF16)
    k = proj(1)
    k_ref[...] = k
    kb_ref[...] = k.astype(BF16)
    v = proj(2)
    v_ref[...] = v
    vb_ref[...] = v.astype(BF16)
    u_ref[...] = _gelu(proj(3))
    vf = _gelu(proj(4))
    mu = jnp.mean(vf, axis=-1, keepdims=True)
    var = jnp.mean(jnp.square(vf - mu), axis=-1, keepdims=True)
    vn_ref[...] = (vf - mu) * lax.rsqrt(var + EPS) * gmg_ref[...]
    mq_ref[...] = proj(5)
    f = jnp.dot(hb, wf_ref[...], preferred_element_type=F32) + bf_ref[...]
    lf_ref[...] = jnp.minimum(f, 0.0) - jnp.log1p(jnp.exp(-jnp.abs(f)))


def _proj_in(x, g, w_main, w_f, b_f, gm_g):
    n, d = x.shape
    tm = TOKEN_TILE
    row = lambda width: pl.BlockSpec((tm, width), lambda i: (i, 0))
    out_shapes = (
        jax.ShapeDtypeStruct((n, d), BF16),
        jax.ShapeDtypeStruct((n, FOX_WIDTH), BF16),
        jax.ShapeDtypeStruct((n, FOX_WIDTH), F32),
        jax.ShapeDtypeStruct((n, FOX_WIDTH), F32),
        jax.ShapeDtypeStruct((n, FOX_WIDTH), BF16),
        jax.ShapeDtypeStruct((n, FOX_WIDTH), BF16),
        jax.ShapeDtypeStruct((n, LANES), F32),
        jax.ShapeDtypeStruct((n, GM_WIDTH), F32),
        jax.ShapeDtypeStruct((n, GM_WIDTH), F32),
        jax.ShapeDtypeStruct((n, MEM_WIDTH), F32),
    )
    return pl.pallas_call(
        _proj_in_kernel,
        grid=(n // tm,),
        in_specs=[row(d), _const_spec((1, d)), _const_spec(w_main.shape), _const_spec(w_f.shape),
                  _const_spec((1, LANES)), _const_spec((1, GM_WIDTH))],
        out_specs=(row(d), row(FOX_WIDTH), row(FOX_WIDTH), row(FOX_WIDTH), row(FOX_WIDTH),
                   row(FOX_WIDTH), row(LANES), row(GM_WIDTH), row(GM_WIDTH), row(MEM_WIDTH)),
        out_shape=out_shapes,
        compiler_params=_params("parallel"),
        name="proj_in",
    )(x, g, w_main, w_f, b_f, gm_g)


def _cumsum_kernel(lf_ref, o_ref, parts_ref, carry_ref):
    @pl.when(pl.program_id(1) == 0)
    def _():
        carry_ref[...] = jnp.zeros_like(carry_ref)

    tc = lf_ref.shape[-1]
    upper = (lax.broadcasted_iota(jnp.int32, (tc, tc), 0)
             <= lax.broadcasted_iota(jnp.int32, (tc, tc), 1)).astype(F32)
    y = jnp.dot(lf_ref[...], upper, precision=HIGHEST, preferred_element_type=F32) + carry_ref[...]
    carry_ref[...] = y[:, tc - 1:tc]
    y = y * LOG2E
    o_ref[...] = y
    hi = y.astype(BF16).astype(F32)
    mid = (y - hi).astype(BF16).astype(F32)
    parts_ref[0] = hi
    parts_ref[1] = mid
    parts_ref[2] = ((y - hi) - mid).astype(BF16).astype(F32)


def _cumsum_lanes(lf_t):
    b, h, s = lf_t.shape
    tc = min(CUMSUM_TILE, s)
    return pl.pallas_call(
        _cumsum_kernel,
        grid=(b, s // tc),
        in_specs=[pl.BlockSpec((None, h, tc), lambda i, j: (i, 0, j))],
        out_specs=(pl.BlockSpec((None, h, tc), lambda i, j: (i, 0, j)),
                   pl.BlockSpec((None, 3, h, tc), lambda i, j: (i, 0, 0, j))),
        out_shape=(jax.ShapeDtypeStruct((b, h, s), F32), jax.ShapeDtypeStruct((b, 3, h, s), F32)),
        scratch_shapes=[pltpu.VMEM((h, 1), F32)],
        compiler_params=_params("parallel", "arbitrary"),
        name="fox_cumsum",
    )(lf_t)


def _fox_attn_kernel(qi_tab, kj_tab, q_ref, k_ref, fa_ref, v_ref, fr_ref, o_ref,
                     qm_ref, fr_rep, m_ref, l_ref, alpha_ref, shift_ref, tmax_ref, acc_ref, s_ref, p_ref,
                     *, tq, tk):
    p = pl.program_id(2)
    qi = qi_tab[p]
    kj = kj_tab[p]
    lane = lax.broadcasted_iota(jnp.int32, (tq, LANES), 1)
    low = lane < FOX_HEAD_DIM
    n_col = tk // LANES
    n_chunk = tq // ATTN_ROWS

    @pl.when(kj == 0)
    def _():
        q = q_ref[...]
        for hh in range(2):
            qm_ref[hh, :, :LANES] = jnp.where(low if hh == 0 else jnp.logical_not(low), q, jnp.zeros_like(q))
            minus_one = (lane >= 3 * hh) & (lane < 3 * hh + 3)
            qm_ref[hh, :, LANES:] = jnp.where(minus_one, -1.0, 0.0).astype(BF16)
            fr_rep[hh] = jnp.broadcast_to(fr_ref[:, hh:hh + 1], (tq, LANES))
        m_ref[...] = jnp.full_like(m_ref, -jnp.inf)
        l_ref[...] = jnp.zeros_like(l_ref)
        acc_ref[...] = jnp.zeros_like(acc_ref)

    def step(masked):
        sub = lax.broadcasted_iota(jnp.int32, (ATTN_ROWS, LANES), 0)
        lane_r = lax.broadcasted_iota(jnp.int32, (ATTN_ROWS, LANES), 1)
        ka = jnp.concatenate([k_ref[...], fa_ref[...]], axis=1)
        for hh in range(2):
            s_ref[hh] = lax.dot_general(qm_ref[hh], ka, (((1,), (1,)), ((), ())), preferred_element_type=F32)

        def scores(hh, r0):
            ts = []
            for j in range(n_col):
                t = s_ref[hh, pl.ds(r0, ATTN_ROWS), j * LANES:(j + 1) * LANES]
                if masked:
                    keep = (kj * tk + j * LANES + lane_r) <= (qi * tq + r0 + sub)
                    t = jnp.where(keep, t, -jnp.inf)
                ts.append(t)
            return ts

        for hh in range(2):
            def max_body(i, carry, hh=hh):
                r0 = pl.multiple_of(i * ATTN_ROWS, ATTN_ROWS)
                tmax_ref[hh, pl.ds(r0, ATTN_ROWS), :] = functools.reduce(jnp.maximum, scores(hh, r0))
                return carry

            lax.fori_loop(0, n_chunk, max_body, 0, unroll=ATTN_MAX_UNROLL)

        for hh in range(2):
            f_row = fr_rep[hh]
            m_prev = m_ref[hh]
            m_new = jnp.maximum(m_prev, jnp.max(tmax_ref[hh], axis=1, keepdims=True) + f_row)
            m_ref[hh] = m_new
            alpha_ref[hh] = jnp.exp2(m_prev - m_new)
            shift_ref[hh] = m_new - f_row

        v2 = v_ref[...]
        pvs = []
        for hh in range(2):
            def exp_body(i, carry, hh=hh):
                r0 = pl.multiple_of(i * ATTN_ROWS, ATTN_ROWS)
                rows = pl.ds(r0, ATTN_ROWS)
                shift = shift_ref[hh, rows, :]
                prs = [jnp.exp2(t - shift) for t in scores(hh, r0)]
                l_ref[hh, rows, :] = alpha_ref[hh, rows, :] * l_ref[hh, rows, :] + functools.reduce(jnp.add, prs)
                for j in range(n_col):
                    p_ref[hh, rows, j * LANES:(j + 1) * LANES] = prs[j].astype(BF16)
                return carry

            lax.fori_loop(0, n_chunk, exp_body, 0, unroll=ATTN_EXP_UNROLL)
            pvs.append(jnp.dot(p_ref[hh], v2, preferred_element_type=F32))
        acc = acc_ref[...]
        acc_ref[...] = jnp.where(low, alpha_ref[0] * acc + pvs[0], alpha_ref[1] * acc + pvs[1])

    on_diagonal = (kj + 1) * tk > qi * tq + 1
    pl.when(on_diagonal)(lambda: step(True))
    pl.when(jnp.logical_not(on_diagonal))(lambda: step(False))

    @pl.when((kj + 1) * tk >= (qi + 1) * tq)
    def _():
        l0 = jnp.sum(l_ref[0], axis=1, keepdims=True)
        l1 = jnp.sum(l_ref[1], axis=1, keepdims=True)
        o_ref[...] = (acc_ref[...] * jnp.where(low, 1.0 / l0, 1.0 / l1)).astype(BF16)


def _fox_prompt(q, kb, f_aug, vb, f_row, batch, seq):
    tq = min(ATTN_TQ, seq)
    tk = min(ATTN_TK, tq)
    nq = seq // tq
    pairs = [(i, j) for i in range(nq) for j in range(((i + 1) * tq) // tk)]
    qi_tab = jnp.asarray(np.array([a for a, _ in pairs], np.int32))
    kj_tab = jnp.asarray(np.array([c for _, c in pairs], np.int32))
    nqb, nkb = seq // tq, seq // tk
    grid_spec = pltpu.PrefetchScalarGridSpec(
        num_scalar_prefetch=2,
        grid=(batch, FOX_HEADS // 2, len(pairs)),
        in_specs=[
            pl.BlockSpec((tq, LANES), lambda b, h, p, qt, kt: (b * nqb + qt[p], h)),
            pl.BlockSpec((tk, LANES), lambda b, h, p, qt, kt: (b * nkb + kt[p], h)),
            pl.BlockSpec((tk, LANES), lambda b, h, p, qt, kt: (b * nkb + kt[p], h)),
            pl.BlockSpec((tk, LANES), lambda b, h, p, qt, kt: (b * nkb + kt[p], h)),
            pl.BlockSpec((None, None, tq, 2), lambda b, h, p, qt, kt: (b, h, qt[p], 0)),
        ],
        out_specs=pl.BlockSpec((tq, LANES), lambda b, h, p, qt, kt: (b * nqb + qt[p], h)),
        scratch_shapes=[pltpu.VMEM((2, tq, 2 * LANES), BF16),
                        pltpu.VMEM((2, tq, LANES), F32),
                        pltpu.VMEM((2, tq, LANES), F32),
                        pltpu.VMEM((2, tq, LANES), F32),
                        pltpu.VMEM((2, tq, LANES), F32),
                        pltpu.VMEM((2, tq, LANES), F32),
                        pltpu.VMEM((2, tq, LANES), F32),
                        pltpu.VMEM((tq, LANES), F32),
                        pltpu.VMEM((2, tq, tk), F32),
                        pltpu.VMEM((2, tq, tk), BF16)],
    )
    return pl.pallas_call(
        functools.partial(_fox_attn_kernel, tq=tq, tk=tk),
        grid_spec=grid_spec,
        out_shape=jax.ShapeDtypeStruct((batch * seq, FOX_WIDTH), BF16),
        compiler_params=_params("parallel", "parallel", "arbitrary"),
        name="fox_prompt_attn",
    )(qi_tab, kj_tab, q, kb, f_aug, vb, f_row)


def _fox_sample_kernel(pt_ref, q_ref, kn_ref, vn_ref, lfn_ref, *refs, n_pages, t_new):
    k_refs = refs[:n_pages]
    v_refs = refs[n_pages:2 * n_pages]
    lf_refs = refs[2 * n_pages:3 * n_pages]
    o_ref = refs[3 * n_pages]
    t_ref = refs[3 * n_pages + 1]
    rows = FOX_HEADS * t_new

    q = q_ref[...]
    qt = jnp.concatenate([q] * FOX_HEADS, axis=0)
    r_w = lax.broadcasted_iota(jnp.int32, (rows, FOX_WIDTH), 0)
    c_w = lax.broadcasted_iota(jnp.int32, (rows, FOX_WIDTH), 1)
    head_match = _div(r_w, t_new) == _div(c_w, FOX_HEAD_DIM)
    q_bd = jnp.where(head_match, qt, 0.0).astype(BF16)
    upper = (lax.broadcasted_iota(jnp.int32, (PAGE_SIZE, PAGE_SIZE), 0)
             <= lax.broadcasted_iota(jnp.int32, (PAGE_SIZE, PAGE_SIZE), 1)).astype(F32)
    r_p = lax.broadcasted_iota(jnp.int32, (rows, PAGE_SIZE), 0)
    c_p = lax.broadcasted_iota(jnp.int32, (rows, PAGE_SIZE), 1)
    pad = jnp.zeros((PAGE_SIZE - t_new, FOX_WIDTH), F32)

    carry = jnp.zeros((FOX_HEADS, 1), F32)
    m = jnp.full((rows, 1), -jnp.inf, F32)
    f_row = None
    for pg in range(n_pages + 1):
        if pg < n_pages:
            kp = k_refs[pg][...]
            lf_t = lf_refs[pg][...]
        else:
            kp = jnp.concatenate([kn_ref[...], pad], axis=0)
            lf_t = lfn_ref[...]
        s = lax.dot_general(q_bd, kp.astype(BF16), (((1,), (1,)), ((), ())), preferred_element_type=F32)
        f_t = jnp.dot(lf_t, upper, precision=HIGHEST, preferred_element_type=F32) + carry
        carry = f_t[:, PAGE_SIZE - 1:PAGE_SIZE]
        f_t = f_t * LOG2E
        f_col = jnp.concatenate(
            [jnp.broadcast_to(f_t[h:h + 1, :], (t_new, PAGE_SIZE)) for h in range(FOX_HEADS)], axis=0)
        t = s - f_col
        if pg == n_pages:
            t = jnp.where(c_p <= _mod(r_p, t_new), t, -jnp.inf)
            f_row = jnp.sum(jnp.where(c_p == _mod(r_p, t_new), f_col, 0.0), axis=1, keepdims=True)
        t_ref[pg] = t
        m = jnp.maximum(m, jnp.max(t, axis=1, keepdims=True))
    m_logit = m + f_row
    shift = m_logit - f_row
    l = jnp.zeros((rows, 1), F32)
    acc = jnp.zeros((rows, FOX_WIDTH), F32)
    for pg in range(n_pages + 1):
        pr = jnp.exp2(t_ref[pg] - shift)
        l = l + jnp.sum(pr, axis=1, keepdims=True)
        vp = v_refs[pg][...] if pg < n_pages else jnp.concatenate([vn_ref[...], pad], axis=0)
        acc = acc + jnp.dot(pr.astype(BF16), vp.astype(BF16), preferred_element_type=F32)
    o = acc / l
    c_o = lax.broadcasted_iota(jnp.int32, (t_new, FOX_WIDTH), 1)
    out = jnp.zeros((t_new, FOX_WIDTH), F32)
    for h in range(FOX_HEADS):
        out = jnp.where(_div(c_o, FOX_HEAD_DIM) == h, o[h * t_new:(h + 1) * t_new, :], out)
    o_ref[...] = out


def _fox_sample(page_table, q32, k32, v32, lf_new_t, cache_k, cache_v, cache_lf_t, layer, row0, t_new):
    nb, n_pages = page_table.shape
    blk0 = row0 // t_new
    tok = pl.BlockSpec((t_new, FOX_WIDTH), lambda b, pt: (blk0 + b, 0))

    def page_spec(shape, pg):
        return pl.BlockSpec((None, None) + shape, lambda b, pt: (layer, pt[b, pg], 0, 0))

    in_specs = [tok, tok, tok, pl.BlockSpec((None, FOX_HEADS, LANES), lambda b, pt: (b, 0, 0))]
    in_specs += [page_spec((PAGE_SIZE, FOX_WIDTH), pg) for pg in range(n_pages)]
    in_specs += [page_spec((PAGE_SIZE, FOX_WIDTH), pg) for pg in range(n_pages)]
    in_specs += [page_spec((FOX_HEADS, PAGE_SIZE), pg) for pg in range(n_pages)]
    grid_spec = pltpu.PrefetchScalarGridSpec(
        num_scalar_prefetch=1,
        grid=(nb,),
        in_specs=in_specs,
        out_specs=pl.BlockSpec((t_new, FOX_WIDTH), lambda b, pt: (b, 0)),
        scratch_shapes=[pltpu.VMEM((n_pages + 1, FOX_HEADS * t_new, PAGE_SIZE), F32)],
    )
    return pl.pallas_call(
        functools.partial(_fox_sample_kernel, n_pages=n_pages, t_new=t_new),
        grid_spec=grid_spec,
        out_shape=jax.ShapeDtypeStruct((nb * t_new, FOX_WIDTH), F32),
        compiler_params=_params("parallel"),
        name="fox_sample_attn",
    )(page_table, q32, k32, v32, lf_new_t, *([cache_k] * n_pages), *([cache_v] * n_pages),
      *([cache_lf_t] * n_pages))


def _norm_matmul_kernel(x_ref, g_ref, w_ref, o_ref):
    hb = _rmsnorm(x_ref[...], g_ref[...]).astype(BF16)
    o_ref[...] = jnp.dot(hb, w_ref[...], preferred_element_type=F32)


def _norm_matmul(x, g, w):
    n, d = x.shape
    tm = min(TOKEN_TILE, n)
    return pl.pallas_call(
        _norm_matmul_kernel,
        grid=(n // tm,),
        in_specs=[pl.BlockSpec((tm, d), lambda i: (i, 0)), _const_spec((1, d)), _const_spec(w.shape)],
        out_specs=pl.BlockSpec((tm, w.shape[1]), lambda i: (i, 0)),
        out_shape=jax.ShapeDtypeStruct((n, w.shape[1]), F32),
        compiler_params=_params("parallel"),
        name="mem_kv",
    )(x, g, w)


def _mem_attend_kernel(q_ref, mk_ref, mv_ref, o_ref):
    scale = MEM_HEAD_DIM ** -0.5
    for h in range(MEM_HEADS):
        sl = slice(h * MEM_HEAD_DIM, (h + 1) * MEM_HEAD_DIM)
        s = lax.dot_general(q_ref[:, sl].astype(BF16), mk_ref[:, sl].astype(BF16),
                            (((1,), (1,)), ((), ())), preferred_element_type=F32) * scale
        e = jnp.exp(s - jnp.max(s, axis=1, keepdims=True))
        pr = e / jnp.sum(e, axis=1, keepdims=True)
        o_ref[:, sl] = jnp.dot(pr.astype(BF16), mv_ref[:, sl].astype(BF16), preferred_element_type=F32)


def _mem_attend(mq, mk, mv, row0, n_mem, t_per_mem, name):
    tm = min(TOKEN_TILE * 2, t_per_mem)
    tiles = t_per_mem // tm
    blk0 = row0 // tm
    m_len = mk.shape[1]
    mem_spec = pl.BlockSpec((None, m_len, MEM_WIDTH), lambda b, i: (b, 0, 0))
    return pl.pallas_call(
        _mem_attend_kernel,
        grid=(n_mem, tiles),
        in_specs=[pl.BlockSpec((tm, MEM_WIDTH), lambda b, i: (blk0 + b * tiles + i, 0)), mem_spec, mem_spec],
        out_specs=pl.BlockSpec((tm, MEM_WIDTH), lambda b, i: (b * tiles + i, 0)),
        out_shape=jax.ShapeDtypeStruct((n_mem * t_per_mem, MEM_WIDTH), F32),
        compiler_params=_params("parallel", "parallel"),
        name=name,
    )(mq, mk, mv)


def _merge_kernel(x_ref, h_ref, fox_ref, u_ref, vn_ref, mem_ref, ws_ref, bs_ref,
                  wg_ref, bg_ref, wbr_ref, wo_ref, o_ref, gm_ref, *, chunk):
    tm, d = x_ref.shape
    r_c = lax.broadcasted_iota(jnp.int32, (CHUNK, CHUNK), 0)
    c_c = lax.broadcasted_iota(jnp.int32, (CHUNK, CHUNK), 1)
    keep = (c_c <= r_c) & (_div(r_c, chunk) == _div(c_c, chunk))
    for g in range(GM_GROUPS):
        w = jnp.where(keep, ws_ref[g], 0.0).astype(BF16)
        sl = slice(g * GM_GROUP_DIM, (g + 1) * GM_GROUP_DIM)
        for c in range(tm // CHUNK):
            rs = slice(c * CHUNK, (c + 1) * CHUNK)
            mix = jnp.dot(w, vn_ref[rs, sl].astype(BF16), preferred_element_type=F32) + bs_ref[:, g:g + 1]
            gm_ref[rs, sl] = (u_ref[rs, sl] * mix).astype(BF16)

    hb = h_ref[...]
    branches = (fox_ref[...].astype(BF16), gm_ref[...], mem_ref[...].astype(BF16))
    merged = None
    for r in range(N_BRANCH):
        gate = jax.nn.sigmoid(jnp.dot(hb, wg_ref[:, r * d:(r + 1) * d], preferred_element_type=F32)
                              + bg_ref[:, r * d:(r + 1) * d])
        term = gate * jnp.dot(branches[r], wbr_ref[r], preferred_element_type=F32)
        merged = term if merged is None else merged + term
    o_ref[...] = x_ref[...] + jnp.dot(merged.astype(BF16), wo_ref[...], preferred_element_type=F32)


def _merge(x, h, fox_o, u, vn, mem_o, ws, bs_t, w_gate, b_gate, w_br, w_o, tile0, ntiles, chunk, name):
    d = x.shape[1]
    tm = TOKEN_TILE
    row = lambda width: pl.BlockSpec((tm, width), lambda i: (tile0 + i, 0))
    own = lambda width: pl.BlockSpec((tm, width), lambda i: (i, 0))
    in_specs = [row(d), row(d), own(FOX_WIDTH), row(GM_WIDTH), row(GM_WIDTH), own(MEM_WIDTH),
                _const_spec(ws.shape), _const_spec(bs_t.shape),
                _const_spec(w_gate.shape), _const_spec((1, N_BRANCH * d)), _const_spec(w_br.shape),
                _const_spec(w_o.shape)]
    return pl.pallas_call(
        functools.partial(_merge_kernel, chunk=chunk),
        grid=(ntiles,),
        in_specs=in_specs,
        out_specs=pl.BlockSpec((tm, d), lambda i: (i, 0)),
        out_shape=jax.ShapeDtypeStruct((ntiles * tm, d), F32),
        scratch_shapes=[pltpu.VMEM((tm, GM_WIDTH), BF16)],
        compiler_params=_params("parallel"),
        name=name,
    )(x, h, fox_o, u, vn, mem_o, ws, bs_t, w_gate, b_gate, w_br, w_o)


def _moe_route_kernel(x_ref, g_ref, wr_ref, br_ref, hn_ref, meta_ref, cnt_ref, carry_ref):
    @pl.when(pl.program_id(0) == 0)
    def _():
        carry_ref[...] = jnp.zeros_like(carry_ref)

    tm = x_ref.shape[0]
    hn = _rmsnorm(x_ref[...], g_ref[...])
    hn_ref[...] = hn
    logits = jnp.dot(hn, wr_ref[...], precision=HIGHEST, preferred_element_type=F32) + br_ref[...]
    col = lax.broadcasted_iota(jnp.int32, (tm, LANES), 1)
    colf = col.astype(F32)
    big = jnp.float32(1e9)

    def masked_softmax(mask):
        z = jnp.where(mask, logits, -jnp.inf)
        e = jnp.exp(z - jnp.max(z, axis=1, keepdims=True))
        return e / jnp.sum(e, axis=1, keepdims=True)

    def first_argmax(vals, vmax, mask):
        return jnp.min(jnp.where(mask & (vals == vmax), colf, big), axis=1, keepdims=True)

    is_group = col < N_GROUPS
    g_prob = masked_softmax(is_group)
    p_g = jnp.max(g_prob, axis=1, keepdims=True)
    g_idx = first_argmax(g_prob, p_g, is_group)
    e_col = col - N_GROUPS
    in_group = (e_col >= 0) & (e_col < N_EXPERTS) & (_div(e_col, EXPERTS_PER_GROUP).astype(F32) == g_idx)
    e_prob = masked_softmax(in_group)
    p1 = jnp.max(e_prob, axis=1, keepdims=True)
    i1 = first_argmax(e_prob, p1, in_group)
    rest = in_group & (colf != i1)
    p2 = jnp.max(jnp.where(rest, e_prob, -1.0), axis=1, keepdims=True)
    i2 = first_argmax(e_prob, p2, rest)
    denom = p1 + p2
    w1 = p1 / denom * p_g
    w2 = p2 / denom * p_g
    e1 = i1 - N_GROUPS
    e2 = i2 - N_GROUPS

    onehot = ((colf == e1) | (colf == e2)).astype(F32)
    lower = (lax.broadcasted_iota(jnp.int32, (tm, tm), 1)
             < lax.broadcasted_iota(jnp.int32, (tm, tm), 0)).astype(BF16)
    before = jnp.dot(lower, onehot.astype(BF16), preferred_element_type=F32) + carry_ref[...]
    r1 = jnp.sum(jnp.where(colf == e1, before, 0.0), axis=1, keepdims=True)
    r2 = jnp.sum(jnp.where(colf == e2, before, 0.0), axis=1, keepdims=True)
    carry_ref[...] = carry_ref[...] + jnp.sum(onehot, axis=0, keepdims=True)
    cnt_ref[...] = jnp.broadcast_to(carry_ref[...], cnt_ref.shape)

    meta = jnp.zeros((tm, LANES), F32)
    for lane_id, val in enumerate((e1, e2, r1, r2, w1, w2)):
        meta = jnp.where(col == lane_id, val, meta)
    meta_ref[...] = meta


def _moe_route(x, g, w_route, b_route):
    n, d = x.shape
    tm = ROUTE_TILE
    return pl.pallas_call(
        _moe_route_kernel,
        grid=(n // tm,),
        in_specs=[pl.BlockSpec((tm, d), lambda i: (i, 0)), _const_spec((1, d)),
                  _const_spec(w_route.shape), _const_spec((1, LANES))],
        out_specs=(pl.BlockSpec((tm, d), lambda i: (i, 0)), pl.BlockSpec((tm, LANES), lambda i: (i, 0)),
                   _const_spec((8, LANES))),
        out_shape=(jax.ShapeDtypeStruct((n, d), F32), jax.ShapeDtypeStruct((n, LANES), F32),
                   jax.ShapeDtypeStruct((8, LANES), F32)),
        scratch_shapes=[pltpu.VMEM((1, LANES), F32)],
        compiler_params=_params("arbitrary"),
        name="moe_route",
    )(x, g, w_route, b_route)


def _moe_expert_kernel(te_ref, tv_ref, src_ref, src_next_ref, hn_hbm, wgu_ref, wdn_ref, ys_ref,
                       x_buf, wgu_bf, wdn_bf, sem, *, tg):
    t = pl.program_id(0)
    n_t = pl.num_programs(0)
    slot = t % 2

    def gather(idx_ref, buf_slot):
        def issue(r, c):
            pltpu.make_async_copy(hn_hbm.at[pl.ds(idx_ref[0, r], 1)], x_buf.at[buf_slot, pl.ds(r, 1)],
                                  sem.at[buf_slot]).start()
            return c

        lax.fori_loop(0, tg, issue, 0, unroll=DMA_UNROLL)

    @pl.when((t == 0) & (tv_ref[0] == 1))
    def _():
        gather(src_ref, 0)

    @pl.when((t + 1 < n_t) & (tv_ref[jnp.minimum(t + 1, n_t - 1)] == 1))
    def _():
        gather(src_next_ref, 1 - slot)

    @pl.when(tv_ref[t] == 1)
    def _():
        @pl.when((t == 0) | (te_ref[t] != te_ref[jnp.maximum(t - 1, 0)]))
        def _():
            wgu_bf[...] = wgu_ref[...].astype(BF16)
            wdn_bf[...] = wdn_ref[...].astype(BF16)

        pltpu.make_async_copy(hn_hbm.at[pl.ds(0, tg)], x_buf.at[slot], sem.at[slot]).wait()
        gu = jnp.dot(x_buf[slot].astype(BF16), wgu_bf[...], preferred_element_type=F32)
        gate = gu[:, :EXPERT_FF]
        act = gate * jax.nn.sigmoid(gate) * gu[:, EXPERT_FF:]
        ys_ref[...] = jnp.dot(act.astype(BF16), wdn_bf[...], preferred_element_type=F32)

    @pl.when(tv_ref[t] != 1)
    def _():
        ys_ref[...] = jnp.zeros_like(ys_ref)


def _moe_experts(tile_expert, tile_valid, src_idx, hn, w_gu, w_dn, layer):
    n_tiles = tile_expert.shape[0]
    tg = MOE_TILE
    d = hn.shape[1]
    grid_spec = pltpu.PrefetchScalarGridSpec(
        num_scalar_prefetch=2,
        grid=(n_tiles,),
        in_specs=[
            pl.BlockSpec((None, 1, tg), lambda t, te, tv: (t, 0, 0), memory_space=pltpu.SMEM),
            pl.BlockSpec((None, 1, tg), lambda t, te, tv: (jnp.minimum(t + 1, n_tiles - 1), 0, 0),
                         memory_space=pltpu.SMEM),
            pl.BlockSpec(memory_space=pl.ANY),
            pl.BlockSpec((None, None, d, 2 * EXPERT_FF), lambda t, te, tv: (layer, te[t], 0, 0)),
            pl.BlockSpec((None, None, EXPERT_FF, d), lambda t, te, tv: (layer, te[t], 0, 0)),
        ],
        out_specs=pl.BlockSpec((tg, d), lambda t, te, tv: (t, 0)),
        scratch_shapes=[pltpu.VMEM((2, tg, d), F32), pltpu.VMEM((d, 2 * EXPERT_FF), BF16),
                        pltpu.VMEM((EXPERT_FF, d), BF16), pltpu.SemaphoreType.DMA((2,))],
    )
    return pl.pallas_call(
        functools.partial(_moe_expert_kernel, tg=tg),
        grid_spec=grid_spec,
        out_shape=jax.ShapeDtypeStruct((n_tiles * tg, d), F32),
        compiler_params=_params("arbitrary"),
        name="moe_experts",
    )(tile_expert, tile_valid, src_idx, src_idx, hn, w_gu, w_dn)


def _moe_combine_kernel(dest_ref, x_ref, meta_ref, fg_ref, ys_hbm, o_ref, buf, sem, *, tm, final):
    def issue(r, c):
        for slot in range(2):
            pltpu.make_async_copy(ys_hbm.at[pl.ds(dest_ref[0, 2 * r + slot], 1)],
                                  buf.at[slot, pl.ds(r, 1)], sem.at[slot]).start()
        return c

    lax.fori_loop(0, tm, issue, 0, unroll=DMA_UNROLL)
    for slot in range(2):
        pltpu.make_async_copy(ys_hbm.at[pl.ds(0, tm)], buf.at[slot], sem.at[slot]).wait()
    meta = meta_ref[...]
    y = x_ref[...] + (meta[:, 4:5] * buf[0] + meta[:, 5:6] * buf[1])
    if final:
        y = _rmsnorm(y, fg_ref[...])
    o_ref[...] = y


def _moe_combine(dest, x, meta, final_g, ys, final):
    n, d = x.shape
    tm = TOKEN_TILE
    return pl.pallas_call(
        functools.partial(_moe_combine_kernel, tm=tm, final=final),
        grid=(n // tm,),
        in_specs=[pl.BlockSpec((None, 1, 2 * tm), lambda i: (i, 0, 0), memory_space=pltpu.SMEM),
                  pl.BlockSpec((tm, d), lambda i: (i, 0)), pl.BlockSpec((tm, LANES), lambda i: (i, 0)),
                  _const_spec((1, d)), pl.BlockSpec(memory_space=pl.ANY)],
        out_specs=pl.BlockSpec((tm, d), lambda i: (i, 0)),
        out_shape=jax.ShapeDtypeStruct((n, d), F32),
        scratch_shapes=[pltpu.VMEM((2, tm, d), F32), pltpu.SemaphoreType.DMA((2,))],
        compiler_params=_params("arbitrary"),
        name="moe_combine",
    )(dest, x, meta, final_g, ys)


def _moe_layer(x, g, w_route, b_route, w_gu, w_dn, layer, final_g, final):
    n, d = x.shape
    tg = MOE_TILE
    hn, meta, counts = _moe_route(x, g, w_route, b_route)
    counts = counts[0, :N_EXPERTS].astype(jnp.int32)
    padded = ((counts + tg - 1) // tg) * tg
    ends = jnp.cumsum(padded)
    starts = ends - padded
    eid = meta[:, 0:2].astype(jnp.int32)
    rank = meta[:, 2:4].astype(jnp.int32)
    dest = starts[eid] + rank
    n_tiles = (2 * n) // tg + N_EXPERTS
    tile_start = jnp.arange(n_tiles, dtype=jnp.int32) * tg
    tile_expert = jnp.minimum(jnp.sum((ends[None, :] <= tile_start[:, None]).astype(jnp.int32), axis=1),
                              N_EXPERTS - 1)
    tile_valid = (tile_start < ends[-1]).astype(jnp.int32)
    token = jnp.broadcast_to(jnp.arange(n, dtype=jnp.int32)[:, None], (n, 2))
    src_idx = jnp.zeros((n_tiles * tg,), jnp.int32).at[dest.reshape(-1)].set(token.reshape(-1))
    ys = _moe_experts(tile_expert, tile_valid, src_idx.reshape(n_tiles, 1, tg), hn, w_gu, w_dn, layer)
    return _moe_combine(dest.reshape(n // TOKEN_TILE, 1, 2 * TOKEN_TILE), x, meta, final_g, ys, final)


def kernel(x_prompt, x_sample, cache_k, cache_v, cache_logf, cache_mem_k, cache_mem_v, page_table, mem_prompt, norm1_g, w_in, b_f, gm_norm_g, gm_ws, gm_bs, mem_norm_g, w_mem_kv, w_gate, b_gate, w_br, w_o, norm2_g, w_rg, b_rg, w_re, b_re, w_moe_gu, w_moe_dn, final_g):
    batch, seq, d = x_prompt.shape
    dec_batch, dec_seq, _ = x_sample.shape
    depth = w_in.shape[0]
    n_pool = cache_k.shape[1]
    mem_len = mem_prompt.shape[1]
    n_p = batch * seq
    n_s = dec_batch * dec_seq
    assert n_p % (2 * TOKEN_TILE) == 0 and n_s % (2 * TOKEN_TILE) == 0 and seq % CHUNK == 0
    assert CHUNK % dec_seq == 0 and (n_p + n_s) % ROUTE_TILE == 0
    n_prompt_tiles = n_p // TOKEN_TILE

    x = jnp.concatenate([x_prompt.reshape(n_p, d), x_sample.reshape(n_s, d)], axis=0)
    mem_flat = mem_prompt.reshape(batch * mem_len, d)
    ck = cache_k.reshape(depth, n_pool, PAGE_SIZE, FOX_WIDTH)
    cv = cache_v.reshape(depth, n_pool, PAGE_SIZE, FOX_WIDTH)
    clf_t = jnp.swapaxes(cache_logf, 2, 3)

    q_end, f_end = 3 * FOX_WIDTH, 3 * FOX_WIDTH + FOX_HEADS
    reps = CHUNK // dec_seq
    outs = {name: [] for name in ("kp", "vp", "fp", "mkp", "mvp", "ks", "vs", "fs", "gs")}
    for l in range(depth):
        w_main = jnp.concatenate([w_in[l][:, :q_end], w_in[l][:, f_end:]], axis=1).astype(BF16)
        w_f = jnp.pad(w_in[l][:, q_end:f_end], ((0, 0), (0, LANES - FOX_HEADS))).astype(BF16)
        bf_pad = jnp.pad(b_f[l], (0, LANES - FOX_HEADS)).reshape(1, LANES)
        ws_sample = jnp.tile(gm_ws[l][:, :dec_seq, :dec_seq], (1, reps, reps))
        bs_sample_t = jnp.tile(gm_bs[l][:, :dec_seq], (1, reps)).T
        w_route = jnp.pad(jnp.concatenate([w_rg[l], w_re[l]], axis=1),
                          ((0, 0), (0, LANES - N_GROUPS - N_EXPERTS)))
        b_route = jnp.pad(jnp.concatenate([b_rg[l], b_re[l]]), (0, LANES - N_GROUPS - N_EXPERTS)).reshape(1, LANES)

        h, q, k32, v32, kb, vb, lf, u, vn, mq = _proj_in(
            x, norm1_g[l].reshape(1, d), w_main, w_f, bf_pad, gm_norm_g[l].reshape(1, GM_WIDTH))

        lf_p = lf[:n_p, :FOX_HEADS].reshape(batch, seq, FOX_HEADS)
        f_t, f_parts = _cumsum_lanes(jnp.swapaxes(lf_p, 1, 2))
        f_row = jnp.swapaxes(f_t.reshape(batch, FOX_HEADS // 2, 2, seq), 2, 3)
        f_aug = jnp.transpose(f_parts, (0, 3, 2, 1)).reshape(batch, seq, FOX_HEADS // 2, 6)
        f_aug = jnp.pad(f_aug, ((0, 0), (0, 0), (0, 0), (0, LANES - 6))).astype(BF16).reshape(n_p, 4 * LANES)
        fox_p = _fox_prompt(q, kb, f_aug, vb, f_row, batch, seq)

        lf_s = lf[n_p:, :FOX_HEADS].reshape(dec_batch, dec_seq, FOX_HEADS)
        lf_new_t = jnp.pad(jnp.swapaxes(lf_s, 1, 2), ((0, 0), (0, 0), (0, LANES - dec_seq)))
        fox_s = _fox_sample(page_table, q.astype(F32), k32, v32, lf_new_t, ck, cv, clf_t, l, n_p, dec_seq)

        kv = _norm_matmul(mem_flat, mem_norm_g[l].reshape(1, d), w_mem_kv[l].astype(BF16))
        mk_p = kv[:, :MEM_WIDTH].reshape(batch, mem_len, MEM_WIDTH)
        mv_p = kv[:, MEM_WIDTH:].reshape(batch, mem_len, MEM_WIDTH)
        mem_p = _mem_attend(mq, mk_p, mv_p, 0, batch, seq, "mem_attend_prompt")
        mem_s = _mem_attend(mq, cache_mem_k[l].reshape(dec_batch, mem_len, MEM_WIDTH),
                            cache_mem_v[l].reshape(dec_batch, mem_len, MEM_WIDTH), n_p, dec_batch, dec_seq,
                            "mem_attend_sample")

        merge_w = (w_gate[l].astype(BF16), b_gate[l].reshape(1, N_BRANCH * d), w_br[l].astype(BF16),
                   w_o[l].astype(BF16))
        x_p = _merge(x, h, fox_p, u, vn, mem_p, gm_ws[l], gm_bs[l].T, *merge_w,
                     0, n_prompt_tiles, CHUNK, "merge_prompt")
        x_s = _merge(x, h, fox_s, u, vn, mem_s, ws_sample, bs_sample_t, *merge_w,
                     n_prompt_tiles, n_s // TOKEN_TILE, dec_seq, "merge_sample")
        x = jnp.concatenate([x_p, x_s], axis=0)
        x = _moe_layer(x, norm2_g[l].reshape(1, d), w_route, b_route, w_moe_gu, w_moe_dn, l,
                       final_g.reshape(1, d), l == depth - 1)

        outs["kp"].append(k32[:n_p].reshape(batch, seq, FOX_HEADS, FOX_HEAD_DIM))
        outs["vp"].append(v32[:n_p].reshape(batch, seq, FOX_HEADS, FOX_HEAD_DIM))
        outs["fp"].append(lf_p)
        outs["mkp"].append(mk_p.reshape(batch, mem_len, MEM_HEADS, MEM_HEAD_DIM))
        outs["mvp"].append(mv_p.reshape(batch, mem_len, MEM_HEADS, MEM_HEAD_DIM))
        outs["ks"].append(k32[n_p:].reshape(dec_batch, dec_seq, FOX_HEADS, FOX_HEAD_DIM))
        outs["vs"].append(v32[n_p:].reshape(dec_batch, dec_seq, FOX_HEADS, FOX_HEAD_DIM))
        outs["fs"].append(lf_s)
        outs["gs"].append(vn[n_p:].reshape(dec_batch, dec_seq, GM_WIDTH))

    y_prompt = x[:n_p].reshape(batch, seq, d)
    y_sample = x[n_p:].reshape(dec_batch, dec_seq, d)
    return (y_prompt, y_sample) + tuple(
        jnp.stack(outs[name]) for name in ("kp", "vp", "fp", "mkp", "mvp", "ks", "vs", "fs", "gs"))
```

```python
import functools

import numpy as np
import jax
import jax.numpy as jnp
from jax import lax
from jax.experimental import pallas as pl
from jax.experimental.pallas import tpu as pltpu

F32 = jnp.float32
BF16 = jnp.bfloat16
HIGHEST = lax.Precision.HIGHEST

EPS = 1e-6
LOG2E = 1.4426950408889634
FOX_HEADS = 8
FOX_HEAD_DIM = 64
FOX_WIDTH = FOX_HEADS * FOX_HEAD_DIM
GM_GROUPS = 4
GM_GROUP_DIM = 128
GM_WIDTH = GM_GROUPS * GM_GROUP_DIM
CHUNK = 128
MEM_HEADS = 4
MEM_HEAD_DIM = 128
MEM_WIDTH = MEM_HEADS * MEM_HEAD_DIM
N_BRANCH = 3
N_GROUPS = 4
EXPERTS_PER_GROUP = 8
N_EXPERTS = N_GROUPS * EXPERTS_PER_GROUP
EXPERT_FF = 512
PAGE_SIZE = 128

LANES = 128
VMEM_LIMIT_BYTES = 56 * 1024 * 1024
TOKEN_TILE = 256
ATTN_TQ = 1024
ATTN_TK = 512
ATTN_ROWS = 16
ATTN_MAX_UNROLL = 64
ATTN_EXP_UNROLL = 64
CUMSUM_TILE = 512
MOE_TILE = 256
ROUTE_TILE = 512
MEM_GROUP_ROWS = 64
DMA_UNROLL = 8


def _params(*sem):
    return pltpu.CompilerParams(dimension_semantics=sem, vmem_limit_bytes=VMEM_LIMIT_BYTES)


def _rmsnorm(x, g):
    return x * lax.rsqrt(jnp.mean(x * x, axis=-1, keepdims=True) + EPS) * g


def _gelu(x):
    return 0.5 * x * (1.0 + jnp.tanh(np.sqrt(2.0 / np.pi).astype(np.float32) * (x + 0.044715 * (x * x * x))))


def _div(x, n):
    assert n & (n - 1) == 0
    return lax.shift_right_arithmetic(x, jnp.int32(n.bit_length() - 1))


def _mod(x, n):
    assert n & (n - 1) == 0
    return x & jnp.int32(n - 1)


def _const_spec(shape):
    zeros = (0,) * len(shape)
    return pl.BlockSpec(shape, lambda *_: zeros)


def _proj_in_kernel(x_ref, g_ref, w_ref, wf_ref, bf_ref, gmg_ref,
                    h_ref, q_ref, k_ref, v_ref, kb_ref, vb_ref, lf_ref, u_ref, vn_ref, mq_ref):
    hb = _rmsnorm(x_ref[...], g_ref[...]).astype(BF16)
    h_ref[...] = hb

    def proj(c):
        return jnp.dot(hb, w_ref[:, c * 512:(c + 1) * 512], preferred_element_type=F32)

    q_ref[...] = (proj(0) * (FOX_HEAD_DIM ** -0.5 * LOG2E)).astype(BF16)
    k = proj(1)
    k_ref[...] = k
    kb_ref[...] = k.astype(BF16)
    v = proj(2)
    v_ref[...] = v
    vb_ref[...] = v.astype(BF16)
    u_ref[...] = _gelu(proj(3))
    vf = _gelu(proj(4))
    mu = jnp.mean(vf, axis=-1, keepdims=True)
    var = jnp.mean(jnp.square(vf - mu), axis=-1, keepdims=True)
    vn_ref[...] = (vf - mu) * lax.rsqrt(var + EPS) * gmg_ref[...]
    mq_ref[...] = proj(5)
    f = jnp.dot(hb, wf_ref[...], preferred_element_type=F32) + bf_ref[...]
    lf_ref[...] = jnp.minimum(f, 0.0) - jnp.log1p(jnp.exp(-jnp.abs(f)))


def _proj_in(x, g, w_main, w_f, b_f, gm_g):
    n, d = x.shape
    tm = TOKEN_TILE
    row = lambda width: pl.BlockSpec((tm, width), lambda i: (i, 0))
    out_shapes = (
        jax.ShapeDtypeStruct((n, d), BF16),
        jax.ShapeDtypeStruct((n, FOX_WIDTH), BF16),
        jax.ShapeDtypeStruct((n, FOX_WIDTH), F32),
        jax.ShapeDtypeStruct((n, FOX_WIDTH), F32),
        jax.ShapeDtypeStruct((n, FOX_WIDTH), BF16),
        jax.ShapeDtypeStruct((n, FOX_WIDTH), BF16),
        jax.ShapeDtypeStruct((n, LANES), F32),
        jax.ShapeDtypeStruct((n, GM_WIDTH), F32),
        jax.ShapeDtypeStruct((n, GM_WIDTH), F32),
        jax.ShapeDtypeStruct((n, MEM_WIDTH), F32),
    )
    return pl.pallas_call(
        _proj_in_kernel,
        grid=(n // tm,),
        in_specs=[row(d), _const_spec((1, d)), _const_spec(w_main.shape), _const_spec(w_f.shape),
                  _const_spec((1, LANES)), _const_spec((1, GM_WIDTH))],
        out_specs=(row(d), row(FOX_WIDTH), row(FOX_WIDTH), row(FOX_WIDTH), row(FOX_WIDTH),
                   row(FOX_WIDTH), row(LANES), row(GM_WIDTH), row(GM_WIDTH), row(MEM_WIDTH)),
        out_shape=out_shapes,
        compiler_params=_params("parallel"),
        name="proj_in",
    )(x, g, w_main, w_f, b_f, gm_g)


def _cumsum_kernel(lf_ref, o_ref, parts_ref, carry_ref):
    @pl.when(pl.program_id(1) == 0)
    def _():
        carry_ref[...] = jnp.zeros_like(carry_ref)

    tc = lf_ref.shape[-1]
    upper = (lax.broadcasted_iota(jnp.int32, (tc, tc), 0)
             <= lax.broadcasted_iota(jnp.int32, (tc, tc), 1)).astype(F32)
    y = jnp.dot(lf_ref[...], upper, precision=HIGHEST, preferred_element_type=F32) + carry_ref[...]
    carry_ref[...] = y[:, tc - 1:tc]
    y = y * LOG2E
    o_ref[...] = y
    hi = y.astype(BF16).astype(F32)
    mid = (y - hi).astype(BF16).astype(F32)
    parts_ref[0] = hi
    parts_ref[1] = mid
    parts_ref[2] = ((y - hi) - mid).astype(BF16).astype(F32)


def _cumsum_lanes(lf_t):
    b, h, s = lf_t.shape
    tc = min(CUMSUM_TILE, s)
    return pl.pallas_call(
        _cumsum_kernel,
        grid=(b, s // tc),
        in_specs=[pl.BlockSpec((None, h, tc), lambda i, j: (i, 0, j))],
        out_specs=(pl.BlockSpec((None, h, tc), lambda i, j: (i, 0, j)),
                   pl.BlockSpec((None, 3, h, tc), lambda i, j: (i, 0, 0, j))),
        out_shape=(jax.ShapeDtypeStruct((b, h, s), F32), jax.ShapeDtypeStruct((b, 3, h, s), F32)),
        scratch_shapes=[pltpu.VMEM((h, 1), F32)],
        compiler_params=_params("parallel", "arbitrary"),
        name="fox_cumsum",
    )(lf_t)


def _fox_attn_kernel(qi_tab, kj_tab, q_ref, k_ref, fa_ref, v_ref, fr_ref, o_ref,
                     qm_ref, fr_rep, m_ref, l_ref, alpha_ref, shift_ref, tmax_ref, acc_ref, s_ref, p_ref,
                     *, tq, tk):
    p = pl.program_id(2)
    qi = qi_tab[p]
    kj = kj_tab[p]
    lane = lax.broadcasted_iota(jnp.int32, (tq, LANES), 1)
    low = lane < FOX_HEAD_DIM
    n_col = tk // LANES
    n_chunk = tq // ATTN_ROWS

    @pl.when(kj == 0)
    def _():
        q = q_ref[...]
        for hh in range(2):
            qm_ref[hh, :, :LANES] = jnp.where(low if hh == 0 else jnp.logical_not(low), q, jnp.zeros_like(q))
            minus_one = (lane >= 3 * hh) & (lane < 3 * hh + 3)
            qm_ref[hh, :, LANES:] = jnp.where(minus_one, -1.0, 0.0).astype(BF16)
            fr_rep[hh] = jnp.broadcast_to(fr_ref[:, hh:hh + 1], (tq, LANES))
        m_ref[...] = jnp.full_like(m_ref, -jnp.inf)
        l_ref[...] = jnp.zeros_like(l_ref)
        acc_ref[...] = jnp.zeros_like(acc_ref)

    def step(masked):
        sub = lax.broadcasted_iota(jnp.int32, (ATTN_ROWS, LANES), 0)
        lane_r = lax.broadcasted_iota(jnp.int32, (ATTN_ROWS, LANES), 1)
        ka = jnp.concatenate([k_ref[...], fa_ref[...]], axis=1)
        for hh in range(2):
            s_ref[hh] = lax.dot_general(qm_ref[hh], ka, (((1,), (1,)), ((), ())), preferred_element_type=F32)

        def scores(hh, r0):
            ts = []
            for j in range(n_col):
                t = s_ref[hh, pl.ds(r0, ATTN_ROWS), j * LANES:(j + 1) * LANES]
                if masked:
                    keep = (kj * tk + j * LANES + lane_r) <= (qi * tq + r0 + sub)
                    t = jnp.where(keep, t, -jnp.inf)
                ts.append(t)
            return ts

        for hh in range(2):
            def max_body(i, carry, hh=hh):
                r0 = pl.multiple_of(i * ATTN_ROWS, ATTN_ROWS)
                tmax_ref[hh, pl.ds(r0, ATTN_ROWS), :] = functools.reduce(jnp.maximum, scores(hh, r0))
                return carry

            lax.fori_loop(0, n_chunk, max_body, 0, unroll=ATTN_MAX_UNROLL)

        for hh in range(2):
            f_row = fr_rep[hh]
            m_prev = m_ref[hh]
            m_new = jnp.maximum(m_prev, jnp.max(tmax_ref[hh], axis=1, keepdims=True) + f_row)
            m_ref[hh] = m_new
            alpha_ref[hh] = jnp.exp2(m_prev - m_new)
            shift_ref[hh] = m_new - f_row

        v2 = v_ref[...]
        pvs = []
        for hh in range(2):
            def exp_body(i, carry, hh=hh):
                r0 = pl.multiple_of(i * ATTN_ROWS, ATTN_ROWS)
                rows = pl.ds(r0, ATTN_ROWS)
                shift = shift_ref[hh, rows, :]
                prs = [jnp.exp2(t - shift) for t in scores(hh, r0)]
                l_ref[hh, rows, :] = alpha_ref[hh, rows, :] * l_ref[hh, rows, :] + functools.reduce(jnp.add, prs)
                for j in range(n_col):
                    p_ref[hh, rows, j * LANES:(j + 1) * LANES] = prs[j].astype(BF16)
                return carry

            lax.fori_loop(0, n_chunk, exp_body, 0, unroll=ATTN_EXP_UNROLL)
            pvs.append(jnp.dot(p_ref[hh], v2, preferred_element_type=F32))
        acc = acc_ref[...]
        acc_ref[...] = jnp.where(low, alpha_ref[0] * acc + pvs[0], alpha_ref[1] * acc + pvs[1])

    on_diagonal = (kj + 1) * tk > qi * tq + 1
    pl.when(on_diagonal)(lambda: step(True))
    pl.when(jnp.logical_not(on_diagonal))(lambda: step(False))

    @pl.when((kj + 1) * tk >= (qi + 1) * tq)
    def _():
        l0 = jnp.sum(l_ref[0], axis=1, keepdims=True)
        l1 = jnp.sum(l_ref[1], axis=1, keepdims=True)
        o_ref[...] = (acc_ref[...] * jnp.where(low, 1.0 / l0, 1.0 / l1)).astype(BF16)


def _fox_prompt(q, kb, f_aug, vb, f_row, batch, seq):
    tq = min(ATTN_TQ, seq)
    tk = min(ATTN_TK, tq)
    nq = seq // tq
    pairs = [(i, j) for i in range(nq) for j in range(((i + 1) * tq) // tk)]
    qi_tab = jnp.asarray(np.array([a for a, _ in pairs], np.int32))
    kj_tab = jnp.asarray(np.array([c for _, c in pairs], np.int32))
    nqb, nkb = seq // tq, seq // tk
    grid_spec = pltpu.PrefetchScalarGridSpec(
        num_scalar_prefetch=2,
        grid=(batch, FOX_HEADS // 2, len(pairs)),
        in_specs=[
            pl.BlockSpec((tq, LANES), lambda b, h, p, qt, kt: (b * nqb + qt[p], h)),
            pl.BlockSpec((tk, LANES), lambda b, h, p, qt, kt: (b * nkb + kt[p], h)),
            pl.BlockSpec((tk, LANES), lambda b, h, p, qt, kt: (b * nkb + kt[p], h)),
            pl.BlockSpec((tk, LANES), lambda b, h, p, qt, kt: (b * nkb + kt[p], h)),
            pl.BlockSpec((None, None, tq, 2), lambda b, h, p, qt, kt: (b, h, qt[p], 0)),
        ],
        out_specs=pl.BlockSpec((tq, LANES), lambda b, h, p, qt, kt: (b * nqb + qt[p], h)),
        scratch_shapes=[pltpu.VMEM((2, tq, 2 * LANES), BF16),
                        pltpu.VMEM((2, tq, LANES), F32),
                        pltpu.VMEM((2, tq, LANES), F32),
                        pltpu.VMEM((2, tq, LANES), F32),
                        pltpu.VMEM((2, tq, LANES), F32),
                        pltpu.VMEM((2, tq, LANES), F32),
                        pltpu.VMEM((2, tq, LANES), F32),
                        pltpu.VMEM((tq, LANES), F32),
                        pltpu.VMEM((2, tq, tk), F32),
                        pltpu.VMEM((2, tq, tk), BF16)],
    )
    return pl.pallas_call(
        functools.partial(_fox_attn_kernel, tq=tq, tk=tk),
        grid_spec=grid_spec,
        out_shape=jax.ShapeDtypeStruct((batch * seq, FOX_WIDTH), BF16),
        compiler_params=_params("parallel", "parallel", "arbitrary"),
        name="fox_prompt_attn",
    )(qi_tab, kj_tab, q, kb, f_aug, vb, f_row)


def _page_cumsum_kernel(lf_ref, fin_ref, tot_ref):
    w = lf_ref.shape[1]
    i = lax.broadcasted_iota(jnp.int32, (w, w), 0)
    j = lax.broadcasted_iota(jnp.int32, (w, w), 1)
    same_head = _mod(i, FOX_HEADS) == _mod(j, FOX_HEADS)
    x = lf_ref[...]
    fin_ref[...] = jnp.dot(x, (same_head & (i <= j)).astype(F32), precision=HIGHEST, preferred_element_type=F32)
    tot_ref[...] = jnp.dot(x, same_head.astype(F32), precision=HIGHEST, preferred_element_type=F32)


def _page_cumsum(lf_flat):
    n, w = lf_flat.shape
    tm = 2 * TOKEN_TILE
    spec = pl.BlockSpec((tm, w), lambda i: (i, 0))
    return pl.pallas_call(
        _page_cumsum_kernel,
        grid=(n // tm,),
        in_specs=[spec],
        out_specs=(spec, spec),
        out_shape=(jax.ShapeDtypeStruct((n, w), F32), jax.ShapeDtypeStruct((n, w), F32)),
        compiler_params=_params("parallel"),
        name="page_cumsum",
    )(lf_flat)


def _fox_sample_kernel(pt_ref, q_ref, kn_ref, vn_ref, lfn_ref, *refs, n_pages, t_new):
    k_refs = refs[:n_pages]
    v_refs = refs[n_pages:2 * n_pages]
    fin_refs = refs[2 * n_pages:3 * n_pages]
    tot_refs = refs[3 * n_pages:4 * n_pages]
    o_ref = refs[4 * n_pages]
    t_ref = refs[4 * n_pages + 1]
    rows = FOX_HEADS * t_new
    width = PAGE_SIZE * FOX_HEADS
    nt = (((1,), (1,)), ((), ()))

    qa = q_ref[...].astype(BF16)
    row = lax.broadcasted_iota(jnp.int32, (rows, width), 0)
    col = lax.broadcasted_iota(jnp.int32, (rows, width), 1)
    valid = _mod(col, FOX_HEADS) == _div(row, t_new)

    carry = jnp.zeros((1, width), F32)
    m = jnp.full((rows, 1), -jnp.inf, F32)
    for pg in range(n_pages):
        k2d = k_refs[pg][...].reshape(width, FOX_HEAD_DIM).astype(BF16)
        s = lax.dot_general(qa, k2d, nt, preferred_element_type=F32)
        f_col = (fin_refs[pg][...] + carry) * LOG2E
        carry = carry + tot_refs[pg][...]
        t = jnp.where(valid, s - f_col, -jnp.inf)
        t_ref[pg] = t
        m = jnp.maximum(m, jnp.max(t, axis=1, keepdims=True))

    r_n = lax.broadcasted_iota(jnp.int32, (rows, LANES), 0)
    c_n = lax.broadcasted_iota(jnp.int32, (rows, LANES), 1)
    i = lax.broadcasted_iota(jnp.int32, (LANES, LANES), 0)
    j = lax.broadcasted_iota(jnp.int32, (LANES, LANES), 1)
    upper = ((_mod(i, FOX_HEADS) == _mod(j, FOX_HEADS)) & (i <= j)).astype(F32)
    f_new = (jnp.dot(lfn_ref[...], upper, precision=HIGHEST, preferred_element_type=F32)[0:1, :]
             + carry[:, :LANES]) * LOG2E
    valid_n = ((c_n < rows) & (_mod(c_n, FOX_HEADS) == _div(r_n, t_new))
               & (_div(c_n, FOX_HEADS) <= _mod(r_n, t_new)))
    s_n = lax.dot_general(qa, kn_ref[...].astype(BF16), nt, preferred_element_type=F32)
    t_n = jnp.where(valid_n, s_n - f_new, -jnp.inf)
    m = jnp.maximum(m, jnp.max(t_n, axis=1, keepdims=True))
    own = c_n == _mod(r_n, t_new) * FOX_HEADS + _div(r_n, t_new)
    f_row = jnp.sum(jnp.where(own, jnp.broadcast_to(f_new, (rows, LANES)), 0.0), axis=1, keepdims=True)
    m_logit = m + f_row
    shift = m_logit - f_row

    pr_n = jnp.exp2(t_n - shift)
    l = jnp.sum(pr_n, axis=1, keepdims=True)
    acc = jnp.dot(pr_n.astype(BF16), vn_ref[...].astype(BF16), preferred_element_type=F32)
    for pg in range(n_pages):
        pr = jnp.exp2(t_ref[pg] - shift)
        l = l + jnp.sum(pr, axis=1, keepdims=True)
        v2d = v_refs[pg][...].reshape(width, FOX_HEAD_DIM).astype(BF16)
        acc = acc + jnp.dot(pr.astype(BF16), v2d, preferred_element_type=F32)
    o_ref[...] = acc / l


def _fox_sample(page_table, q_ht, k_new, v_new, lf_new, cache_k, cache_v, fin, tot, layer, n_pool, t_new):
    nb, n_pages = page_table.shape
    rows = FOX_HEADS * t_new
    width = PAGE_SIZE * FOX_HEADS
    per_batch = lambda shape: pl.BlockSpec((None,) + shape, lambda b, pt: (b, 0, 0))

    def cache_spec(pg):
        return pl.BlockSpec((None, None, PAGE_SIZE, FOX_HEADS, FOX_HEAD_DIM),
                            lambda b, pt: (layer, pt[b, pg], 0, 0, 0))

    def flat_spec(pg):
        return pl.BlockSpec((None, 1, width), lambda b, pt: (layer * n_pool + pt[b, pg], 0, 0))

    in_specs = [per_batch((rows, FOX_HEAD_DIM)), per_batch((LANES, FOX_HEAD_DIM)),
                per_batch((LANES, FOX_HEAD_DIM)), per_batch((8, LANES))]
    in_specs += [cache_spec(pg) for pg in range(n_pages)] * 2
    in_specs += [flat_spec(pg) for pg in range(n_pages)] * 2
    grid_spec = pltpu.PrefetchScalarGridSpec(
        num_scalar_prefetch=1,
        grid=(nb,),
        in_specs=in_specs,
        out_specs=per_batch((rows, FOX_HEAD_DIM)),
        scratch_shapes=[pltpu.VMEM((n_pages, rows, width), F32)],
    )
    return pl.pallas_call(
        functools.partial(_fox_sample_kernel, n_pages=n_pages, t_new=t_new),
        grid_spec=grid_spec,
        out_shape=jax.ShapeDtypeStruct((nb, rows, FOX_HEAD_DIM), F32),
        compiler_params=_params("parallel"),
        name="fox_sample_attn",
    )(page_table, q_ht, k_new, v_new, lf_new, *([cache_k] * n_pages), *([cache_v] * n_pages),
      *([fin] * n_pages), *([tot] * n_pages))


def _norm_matmul_kernel(x_ref, g_ref, w_ref, o_ref):
    hb = _rmsnorm(x_ref[...], g_ref[...]).astype(BF16)
    o_ref[...] = jnp.dot(hb, w_ref[...], preferred_element_type=F32)


def _norm_matmul(x, g, w):
    n, d = x.shape
    tm = min(TOKEN_TILE, n)
    return pl.pallas_call(
        _norm_matmul_kernel,
        grid=(n // tm,),
        in_specs=[pl.BlockSpec((tm, d), lambda i: (i, 0)), _const_spec((1, d)), _const_spec(w.shape)],
        out_specs=pl.BlockSpec((tm, w.shape[1]), lambda i: (i, 0)),
        out_shape=jax.ShapeDtypeStruct((n, w.shape[1]), F32),
        compiler_params=_params("parallel"),
        name="mem_kv",
    )(x, g, w)


def _mem_attend_kernel(q_ref, mk_ref, mv_ref, o_ref):
    scale = MEM_HEAD_DIM ** -0.5
    n_mem = mk_ref.shape[0]
    t = q_ref.shape[0] // n_mem
    for g in range(n_mem):
        rs = slice(g * t, (g + 1) * t)
        for h in range(MEM_HEADS):
            sl = slice(h * MEM_HEAD_DIM, (h + 1) * MEM_HEAD_DIM)
            s = lax.dot_general(q_ref[rs, sl].astype(BF16), mk_ref[g, :, sl].astype(BF16),
                                (((1,), (1,)), ((), ())), preferred_element_type=F32) * scale
            e = jnp.exp(s - jnp.max(s, axis=1, keepdims=True))
            pr = e / jnp.sum(e, axis=1, keepdims=True)
            o_ref[rs, sl] = jnp.dot(pr.astype(BF16), mv_ref[g, :, sl].astype(BF16), preferred_element_type=F32)


def _mem_attend(mq, mk, mv, row0, n_mem, t_per_mem, name):
    tm = min(TOKEN_TILE * 2, t_per_mem)
    tiles = t_per_mem // tm
    group = max(1, min(MEM_GROUP_ROWS // tm, n_mem)) if tiles == 1 else 1
    assert n_mem % group == 0
    rows = tm * group
    blk0 = row0 // rows
    m_len = mk.shape[1]
    mem_spec = pl.BlockSpec((group, m_len, MEM_WIDTH), lambda b, i: (b, 0, 0))
    return pl.pallas_call(
        _mem_attend_kernel,
        grid=(n_mem // group, tiles),
        in_specs=[pl.BlockSpec((rows, MEM_WIDTH), lambda b, i: (blk0 + b * tiles + i, 0)), mem_spec, mem_spec],
        out_specs=pl.BlockSpec((rows, MEM_WIDTH), lambda b, i: (b * tiles + i, 0)),
        out_shape=jax.ShapeDtypeStruct((n_mem * t_per_mem, MEM_WIDTH), F32),
        compiler_params=_params("parallel", "parallel"),
        name=name,
    )(mq, mk, mv)


def _merge_kernel(x_ref, h_ref, fox_ref, u_ref, vn_ref, mem_ref, ws_ref, bs_ref,
                  wg_ref, bg_ref, wbr_ref, wo_ref, o_ref, gm_ref, *, chunk):
    tm, d = x_ref.shape
    r_c = lax.broadcasted_iota(jnp.int32, (CHUNK, CHUNK), 0)
    c_c = lax.broadcasted_iota(jnp.int32, (CHUNK, CHUNK), 1)
    keep = (c_c <= r_c) & (_div(r_c, chunk) == _div(c_c, chunk))
    for g in range(GM_GROUPS):
        w = jnp.where(keep, ws_ref[g], 0.0).astype(BF16)
        sl = slice(g * GM_GROUP_DIM, (g + 1) * GM_GROUP_DIM)
        for c in range(tm // CHUNK):
            rs = slice(c * CHUNK, (c + 1) * CHUNK)
            mix = jnp.dot(w, vn_ref[rs, sl].astype(BF16), preferred_element_type=F32) + bs_ref[:, g:g + 1]
            gm_ref[rs, sl] = (u_ref[rs, sl] * mix).astype(BF16)

    hb = h_ref[...]
    branches = (fox_ref[...].astype(BF16), gm_ref[...], mem_ref[...].astype(BF16))
    merged = None
    for r in range(N_BRANCH):
        gate = jax.nn.sigmoid(jnp.dot(hb, wg_ref[:, r * d:(r + 1) * d], preferred_element_type=F32)
                              + bg_ref[:, r * d:(r + 1) * d])
        term = gate * jnp.dot(branches[r], wbr_ref[r], preferred_element_type=F32)
        merged = term if merged is None else merged + term
    o_ref[...] = x_ref[...] + jnp.dot(merged.astype(BF16), wo_ref[...], preferred_element_type=F32)


def _merge(x, h, fox_o, u, vn, mem_o, ws, bs_t, w_gate, b_gate, w_br, w_o, tile0, ntiles, chunk, name):
    d = x.shape[1]
    tm = TOKEN_TILE
    row = lambda width: pl.BlockSpec((tm, width), lambda i: (tile0 + i, 0))
    own = lambda width: pl.BlockSpec((tm, width), lambda i: (i, 0))
    in_specs = [row(d), row(d), own(FOX_WIDTH), row(GM_WIDTH), row(GM_WIDTH), own(MEM_WIDTH),
                _const_spec(ws.shape), _const_spec(bs_t.shape),
                _const_spec(w_gate.shape), _const_spec((1, N_BRANCH * d)), _const_spec(w_br.shape),
                _const_spec(w_o.shape)]
    return pl.pallas_call(
        functools.partial(_merge_kernel, chunk=chunk),
        grid=(ntiles,),
        in_specs=in_specs,
        out_specs=pl.BlockSpec((tm, d), lambda i: (i, 0)),
        out_shape=jax.ShapeDtypeStruct((ntiles * tm, d), F32),
        scratch_shapes=[pltpu.VMEM((tm, GM_WIDTH), BF16)],
        compiler_params=_params("parallel"),
        name=name,
    )(x, h, fox_o, u, vn, mem_o, ws, bs_t, w_gate, b_gate, w_br, w_o)


def _moe_route_kernel(x_ref, g_ref, wr_ref, br_ref, hn_ref, meta_ref, cnt_ref, carry_ref):
    @pl.when(pl.program_id(0) == 0)
    def _():
        carry_ref[...] = jnp.zeros_like(carry_ref)

    tm = x_ref.shape[0]
    hn = _rmsnorm(x_ref[...], g_ref[...])
    hn_ref[...] = hn
    logits = jnp.dot(hn, wr_ref[...], precision=HIGHEST, preferred_element_type=F32) + br_ref[...]
    col = lax.broadcasted_iota(jnp.int32, (tm, LANES), 1)
    colf = col.astype(F32)
    big = jnp.float32(1e9)

    def masked_softmax(mask):
        z = jnp.where(mask, logits, -jnp.inf)
        e = jnp.exp(z - jnp.max(z, axis=1, keepdims=True))
        return e / jnp.sum(e, axis=1, keepdims=True)

    def first_argmax(vals, vmax, mask):
        return jnp.min(jnp.where(mask & (vals == vmax), colf, big), axis=1, keepdims=True)

    is_group = col < N_GROUPS
    g_prob = masked_softmax(is_group)
    p_g = jnp.max(g_prob, axis=1, keepdims=True)
    g_idx = first_argmax(g_prob, p_g, is_group)
    e_col = col - N_GROUPS
    in_group = (e_col >= 0) & (e_col < N_EXPERTS) & (_div(e_col, EXPERTS_PER_GROUP).astype(F32) == g_idx)
    e_prob = masked_softmax(in_group)
    p1 = jnp.max(e_prob, axis=1, keepdims=True)
    i1 = first_argmax(e_prob, p1, in_group)
    rest = in_group & (colf != i1)
    p2 = jnp.max(jnp.where(rest, e_prob, -1.0), axis=1, keepdims=True)
    i2 = first_argmax(e_prob, p2, rest)
    denom = p1 + p2
    w1 = p1 / denom * p_g
    w2 = p2 / denom * p_g
    e1 = i1 - N_GROUPS
    e2 = i2 - N_GROUPS

    onehot = ((colf == e1) | (colf == e2)).astype(F32)
    lower = (lax.broadcasted_iota(jnp.int32, (tm, tm), 1)
             < lax.broadcasted_iota(jnp.int32, (tm, tm), 0)).astype(BF16)
    before = jnp.dot(lower, onehot.astype(BF16), preferred_element_type=F32) + carry_ref[...]
    r1 = jnp.sum(jnp.where(colf == e1, before, 0.0), axis=1, keepdims=True)
    r2 = jnp.sum(jnp.where(colf == e2, before, 0.0), axis=1, keepdims=True)
    carry_ref[...] = carry_ref[...] + jnp.sum(onehot, axis=0, keepdims=True)
    cnt_ref[...] = jnp.broadcast_to(carry_ref[...], cnt_ref.shape)

    meta = jnp.zeros((tm, LANES), F32)
    for lane_id, val in enumerate((e1, e2, r1, r2, w1, w2)):
        meta = jnp.where(col == lane_id, val, meta)
    meta_ref[...] = meta


def _moe_route(x, g, w_route, b_route):
    n, d = x.shape
    tm = ROUTE_TILE
    return pl.pallas_call(
        _moe_route_kernel,
        grid=(n // tm,),
        in_specs=[pl.BlockSpec((tm, d), lambda i: (i, 0)), _const_spec((1, d)),
                  _const_spec(w_route.shape), _const_spec((1, LANES))],
        out_specs=(pl.BlockSpec((tm, d), lambda i: (i, 0)), pl.BlockSpec((tm, LANES), lambda i: (i, 0)),
                   _const_spec((8, LANES))),
        out_shape=(jax.ShapeDtypeStruct((n, d), F32), jax.ShapeDtypeStruct((n, LANES), F32),
                   jax.ShapeDtypeStruct((8, LANES), F32)),
        scratch_shapes=[pltpu.VMEM((1, LANES), F32)],
        compiler_params=_params("arbitrary"),
        name="moe_route",
    )(x, g, w_route, b_route)


def _moe_expert_kernel(te_ref, tv_ref, src_ref, src_next_ref, hn_hbm, wgu_ref, wdn_ref, ys_ref,
                       x_buf, wgu_bf, wdn_bf, sem, *, tg):
    t = pl.program_id(0)
    n_t = pl.num_programs(0)
    slot = t % 2

    def gather(idx_ref, buf_slot):
        def issue(r, c):
            pltpu.make_async_copy(hn_hbm.at[pl.ds(idx_ref[0, r], 1)], x_buf.at[buf_slot, pl.ds(r, 1)],
                                  sem.at[buf_slot]).start()
            return c

        lax.fori_loop(0, tg, issue, 0, unroll=DMA_UNROLL)

    @pl.when((t == 0) & (tv_ref[0] == 1))
    def _():
        gather(src_ref, 0)

    @pl.when((t + 1 < n_t) & (tv_ref[jnp.minimum(t + 1, n_t - 1)] == 1))
    def _():
        gather(src_next_ref, 1 - slot)

    @pl.when(tv_ref[t] == 1)
    def _():
        @pl.when((t == 0) | (te_ref[t] != te_ref[jnp.maximum(t - 1, 0)]))
        def _():
            wgu_bf[...] = wgu_ref[...].astype(BF16)
            wdn_bf[...] = wdn_ref[...].astype(BF16)

        pltpu.make_async_copy(hn_hbm.at[pl.ds(0, tg)], x_buf.at[slot], sem.at[slot]).wait()
        gu = jnp.dot(x_buf[slot].astype(BF16), wgu_bf[...], preferred_element_type=F32)
        gate = gu[:, :EXPERT_FF]
        act = gate * jax.nn.sigmoid(gate) * gu[:, EXPERT_FF:]
        ys_ref[...] = jnp.dot(act.astype(BF16), wdn_bf[...], preferred_element_type=F32)

    @pl.when(tv_ref[t] != 1)
    def _():
        ys_ref[...] = jnp.zeros_like(ys_ref)


def _moe_experts(tile_expert, tile_valid, src_idx, hn, w_gu, w_dn, layer):
    n_tiles = tile_expert.shape[0]
    tg = MOE_TILE
    d = hn.shape[1]
    grid_spec = pltpu.PrefetchScalarGridSpec(
        num_scalar_prefetch=2,
        grid=(n_tiles,),
        in_specs=[
            pl.BlockSpec((None, 1, tg), lambda t, te, tv: (t, 0, 0), memory_space=pltpu.SMEM),
            pl.BlockSpec((None, 1, tg), lambda t, te, tv: (jnp.minimum(t + 1, n_tiles - 1), 0, 0),
                         memory_space=pltpu.SMEM),
            pl.BlockSpec(memory_space=pl.ANY),
            pl.BlockSpec((None, None, d, 2 * EXPERT_FF), lambda t, te, tv: (layer, te[t], 0, 0)),
            pl.BlockSpec((None, None, EXPERT_FF, d), lambda t, te, tv: (layer, te[t], 0, 0)),
        ],
        out_specs=pl.BlockSpec((tg, d), lambda t, te, tv: (t, 0)),
        scratch_shapes=[pltpu.VMEM((2, tg, d), F32), pltpu.VMEM((d, 2 * EXPERT_FF), BF16),
                        pltpu.VMEM((EXPERT_FF, d), BF16), pltpu.SemaphoreType.DMA((2,))],
    )
    return pl.pallas_call(
        functools.partial(_moe_expert_kernel, tg=tg),
        grid_spec=grid_spec,
        out_shape=jax.ShapeDtypeStruct((n_tiles * tg, d), F32),
        compiler_params=_params("arbitrary"),
        name="moe_experts",
    )(tile_expert, tile_valid, src_idx, src_idx, hn, w_gu, w_dn)


def _moe_combine_kernel(dest_ref, x_ref, meta_ref, fg_ref, ys_hbm, o_ref, buf, sem, *, tm, final):
    def issue(r, c):
        for slot in range(2):
            pltpu.make_async_copy(ys_hbm.at[pl.ds(dest_ref[0, 2 * r + slot], 1)],
                                  buf.at[slot, pl.ds(r, 1)], sem.at[slot]).start()
        return c

    lax.fori_loop(0, tm, issue, 0, unroll=DMA_UNROLL)
    for slot in range(2):
        pltpu.make_async_copy(ys_hbm.at[pl.ds(0, tm)], buf.at[slot], sem.at[slot]).wait()
    meta = meta_ref[...]
    y = x_ref[...] + (meta[:, 4:5] * buf[0] + meta[:, 5:6] * buf[1])
    if final:
        y = _rmsnorm(y, fg_ref[...])
    o_ref[...] = y


def _moe_combine(dest, x, meta, final_g, ys, final):
    n, d = x.shape
    tm = TOKEN_TILE
    return pl.pallas_call(
        functools.partial(_moe_combine_kernel, tm=tm, final=final),
        grid=(n // tm,),
        in_specs=[pl.BlockSpec((None, 1, 2 * tm), lambda i: (i, 0, 0), memory_space=pltpu.SMEM),
                  pl.BlockSpec((tm, d), lambda i: (i, 0)), pl.BlockSpec((tm, LANES), lambda i: (i, 0)),
                  _const_spec((1, d)), pl.BlockSpec(memory_space=pl.ANY)],
        out_specs=pl.BlockSpec((tm, d), lambda i: (i, 0)),
        out_shape=jax.ShapeDtypeStruct((n, d), F32),
        scratch_shapes=[pltpu.VMEM((2, tm, d), F32), pltpu.SemaphoreType.DMA((2,))],
        compiler_params=_params("arbitrary"),
        name="moe_combine",
    )(dest, x, meta, final_g, ys)


def _moe_layer(x, g, w_route, b_route, w_gu, w_dn, layer, final_g, final):
    n, d = x.shape
    tg = MOE_TILE
    hn, meta, counts = _moe_route(x, g, w_route, b_route)
    counts = counts[0, :N_EXPERTS].astype(jnp.int32)
    padded = ((counts + tg - 1) // tg) * tg
    ends = jnp.cumsum(padded)
    starts = ends - padded
    eid = meta[:, 0:2].astype(jnp.int32)
    rank = meta[:, 2:4].astype(jnp.int32)
    dest = starts[eid] + rank
    n_tiles = (2 * n) // tg + N_EXPERTS
    tile_start = jnp.arange(n_tiles, dtype=jnp.int32) * tg
    tile_expert = jnp.minimum(jnp.sum((ends[None, :] <= tile_start[:, None]).astype(jnp.int32), axis=1),
                              N_EXPERTS - 1)
    tile_valid = (tile_start < ends[-1]).astype(jnp.int32)
    token = jnp.broadcast_to(jnp.arange(n, dtype=jnp.int32)[:, None], (n, 2))
    src_idx = jnp.zeros((n_tiles * tg,), jnp.int32).at[dest.reshape(-1)].set(token.reshape(-1))
    ys = _moe_experts(tile_expert, tile_valid, src_idx.reshape(n_tiles, 1, tg), hn, w_gu, w_dn, layer)
    return _moe_combine(dest.reshape(n // TOKEN_TILE, 1, 2 * TOKEN_TILE), x, meta, final_g, ys, final)


def kernel(x_prompt, x_sample, cache_k, cache_v, cache_logf, cache_mem_k, cache_mem_v, page_table, mem_prompt, norm1_g, w_in, b_f, gm_norm_g, gm_ws, gm_bs, mem_norm_g, w_mem_kv, w_gate, b_gate, w_br, w_o, norm2_g, w_rg, b_rg, w_re, b_re, w_moe_gu, w_moe_dn, final_g):
    batch, seq, d = x_prompt.shape
    dec_batch, dec_seq, _ = x_sample.shape
    depth = w_in.shape[0]
    n_pool = cache_k.shape[1]
    mem_len = mem_prompt.shape[1]
    n_p = batch * seq
    n_s = dec_batch * dec_seq
    assert n_p % (2 * TOKEN_TILE) == 0 and n_s % (2 * TOKEN_TILE) == 0 and seq % CHUNK == 0
    assert CHUNK % dec_seq == 0 and (n_p + n_s) % ROUTE_TILE == 0
    n_prompt_tiles = n_p // TOKEN_TILE

    x = jnp.concatenate([x_prompt.reshape(n_p, d), x_sample.reshape(n_s, d)], axis=0)
    mem_flat = mem_prompt.reshape(batch * mem_len, d)
    n_pages_all = depth * n_pool
    pad_pages = -n_pages_all % (2 * TOKEN_TILE)
    lf_flat = jnp.pad(cache_logf.reshape(n_pages_all, PAGE_SIZE * FOX_HEADS), ((0, pad_pages), (0, 0)))
    fin, tot = _page_cumsum(lf_flat)
    fin = fin.reshape(n_pages_all + pad_pages, 1, PAGE_SIZE * FOX_HEADS)
    tot = tot.reshape(n_pages_all + pad_pages, 1, PAGE_SIZE * FOX_HEADS)
    sample_rows = dec_seq * FOX_HEADS

    q_end, f_end = 3 * FOX_WIDTH, 3 * FOX_WIDTH + FOX_HEADS
    reps = CHUNK // dec_seq
    outs = {name: [] for name in ("kp", "vp", "fp", "mkp", "mvp", "ks", "vs", "fs", "gs")}
    for l in range(depth):
        w_main = jnp.concatenate([w_in[l][:, :q_end], w_in[l][:, f_end:]], axis=1).astype(BF16)
        w_f = jnp.pad(w_in[l][:, q_end:f_end], ((0, 0), (0, LANES - FOX_HEADS))).astype(BF16)
        bf_pad = jnp.pad(b_f[l], (0, LANES - FOX_HEADS)).reshape(1, LANES)
        ws_sample = jnp.tile(gm_ws[l][:, :dec_seq, :dec_seq], (1, reps, reps))
        bs_sample_t = jnp.tile(gm_bs[l][:, :dec_seq], (1, reps)).T
        w_route = jnp.pad(jnp.concatenate([w_rg[l], w_re[l]], axis=1),
                          ((0, 0), (0, LANES - N_GROUPS - N_EXPERTS)))
        b_route = jnp.pad(jnp.concatenate([b_rg[l], b_re[l]]), (0, LANES - N_GROUPS - N_EXPERTS)).reshape(1, LANES)

        h, q, k32, v32, kb, vb, lf, u, vn, mq = _proj_in(
            x, norm1_g[l].reshape(1, d), w_main, w_f, bf_pad, gm_norm_g[l].reshape(1, GM_WIDTH))

        lf_p = lf[:n_p, :FOX_HEADS].reshape(batch, seq, FOX_HEADS)
        f_t, f_parts = _cumsum_lanes(jnp.swapaxes(lf_p, 1, 2))
        f_row = jnp.swapaxes(f_t.reshape(batch, FOX_HEADS // 2, 2, seq), 2, 3)
        f_aug = jnp.transpose(f_parts, (0, 3, 2, 1)).reshape(batch, seq, FOX_HEADS // 2, 6)
        f_aug = jnp.pad(f_aug, ((0, 0), (0, 0), (0, 0), (0, LANES - 6))).astype(BF16).reshape(n_p, 4 * LANES)
        fox_p = _fox_prompt(q, kb, f_aug, vb, f_row, batch, seq)

        lf_s = lf[n_p:, :FOX_HEADS].reshape(dec_batch, dec_seq, FOX_HEADS)
        q_ht = jnp.swapaxes(q[n_p:].astype(F32).reshape(dec_batch, dec_seq, FOX_HEADS, FOX_HEAD_DIM), 1, 2)
        q_ht = q_ht.reshape(dec_batch, sample_rows, FOX_HEAD_DIM)
        pad_rows = ((0, 0), (0, LANES - sample_rows), (0, 0))
        k_new = jnp.pad(k32[n_p:].reshape(dec_batch, sample_rows, FOX_HEAD_DIM), pad_rows)
        v_new = jnp.pad(v32[n_p:].reshape(dec_batch, sample_rows, FOX_HEAD_DIM), pad_rows)
        lf_new = jnp.pad(lf_s.reshape(dec_batch, 1, sample_rows), ((0, 0), (0, 0), (0, LANES - sample_rows)))
        lf_new = jnp.broadcast_to(lf_new, (dec_batch, 8, LANES))
        fox_s = _fox_sample(page_table, q_ht, k_new, v_new, lf_new, cache_k, cache_v, fin, tot, l, n_pool, dec_seq)
        fox_s = jnp.swapaxes(fox_s.reshape(dec_batch, FOX_HEADS, dec_seq, FOX_HEAD_DIM), 1, 2).reshape(n_s, FOX_WIDTH)

        kv = _norm_matmul(mem_flat, mem_norm_g[l].reshape(1, d), w_mem_kv[l].astype(BF16))
        mk_p = kv[:, :MEM_WIDTH].reshape(batch, mem_len, MEM_WIDTH)
        mv_p = kv[:, MEM_WIDTH:].reshape(batch, mem_len, MEM_WIDTH)
        mem_p = _mem_attend(mq, mk_p, mv_p, 0, batch, seq, "mem_attend_prompt")
        mem_s = _mem_attend(mq, cache_mem_k[l].reshape(dec_batch, mem_len, MEM_WIDTH),
                            cache_mem_v[l].reshape(dec_batch, mem_len, MEM_WIDTH), n_p, dec_batch, dec_seq,
                            "mem_attend_sample")

        merge_w = (w_gate[l].astype(BF16), b_gate[l].reshape(1, N_BRANCH * d), w_br[l].astype(BF16),
                   w_o[l].astype(BF16))
        x_p = _merge(x, h, fox_p, u, vn, mem_p, gm_ws[l], gm_bs[l].T, *merge_w,
                     0, n_prompt_tiles, CHUNK, "merge_prompt")
        x_s = _merge(x, h, fox_s, u, vn, mem_s, ws_sample, bs_sample_t, *merge_w,
                     n_prompt_tiles, n_s // TOKEN_TILE, dec_seq, "merge_sample")
        x = jnp.concatenate([x_p, x_s], axis=0)
        x = _moe_layer(x, norm2_g[l].reshape(1, d), w_route, b_route, w_moe_gu, w_moe_dn, l,
                       final_g.reshape(1, d), l == depth - 1)

        outs["kp"].append(k32[:n_p].reshape(batch, seq, FOX_HEADS, FOX_HEAD_DIM))
        outs["vp"].append(v32[:n_p].reshape(batch, seq, FOX_HEADS, FOX_HEAD_DIM))
        outs["fp"].append(lf_p)
        outs["mkp"].append(mk_p.reshape(batch, mem_len, MEM_HEADS, MEM_HEAD_DIM))
        outs["mvp"].append(mv_p.reshape(batch, mem_len, MEM_HEADS, MEM_HEAD_DIM))
        outs["ks"].append(k32[n_p:].reshape(dec_batch, dec_seq, FOX_HEADS, FOX_HEAD_DIM))
        outs["vs"].append(v32[n_p:].reshape(dec_batch, dec_seq, FOX_HEADS, FOX_HEAD_DIM))
        outs["fs"].append(lf_s)
        outs["gs"].append(vn[n_p:].reshape(dec_batch, dec_seq, GM_WIDTH))

    y_prompt = x[:n_p].reshape(batch, seq, d)
    y_sample = x[n_p:].reshape(dec_batch, dec_seq, d)
    return (y_prompt, y_sample) + tuple(
        jnp.stack(outs[name]) for name in ("kp", "vp", "fp", "mkp", "mvp", "ks", "vs", "fs", "gs"))
```

```python
import functools

import numpy as np
import jax
import jax.numpy as jnp
from jax import lax
from jax.experimental import pallas as pl
from jax.experimental.pallas import tpu as pltpu

F32 = jnp.float32
BF16 = jnp.bfloat16
HIGHEST = lax.Precision.HIGHEST

EPS = 1e-6
LOG2E = 1.4426950408889634
FOX_HEADS = 8
FOX_HEAD_DIM = 64
FOX_WIDTH = FOX_HEADS * FOX_HEAD_DIM
GM_GROUPS = 4
GM_GROUP_DIM = 128
GM_WIDTH = GM_GROUPS * GM_GROUP_DIM
CHUNK = 128
MEM_HEADS = 4
MEM_HEAD_DIM = 128
MEM_WIDTH = MEM_HEADS * MEM_HEAD_DIM
N_BRANCH = 3
N_GROUPS = 4
EXPERTS_PER_GROUP = 8
N_EXPERTS = N_GROUPS * EXPERTS_PER_GROUP
EXPERT_FF = 512
PAGE_SIZE = 128

LANES = 128
VMEM_LIMIT_BYTES = 56 * 1024 * 1024
TOKEN_TILE = 256
ATTN_TQ = 1024
ATTN_TK = 512
ATTN_ROWS = 16
ATTN_MAX_UNROLL = 64
ATTN_EXP_UNROLL = 64
CUMSUM_TILE = 512
MOE_TILE = 256
ROUTE_TILE = 512
MEM_GROUP_ROWS = 64
DMA_UNROLL = 8


def _params(*sem):
    return pltpu.CompilerParams(dimension_semantics=sem, vmem_limit_bytes=VMEM_LIMIT_BYTES)


def _rmsnorm(x, g):
    return x * lax.rsqrt(jnp.mean(x * x, axis=-1, keepdims=True) + EPS) * g


def _gelu(x):
    return 0.5 * x * (1.0 + jnp.tanh(np.sqrt(2.0 / np.pi).astype(np.float32) * (x + 0.044715 * (x * x * x))))


def _div(x, n):
    assert n & (n - 1) == 0
    return lax.shift_right_arithmetic(x, jnp.int32(n.bit_length() - 1))


def _mod(x, n):
    assert n & (n - 1) == 0
    return x & jnp.int32(n - 1)


def _const_spec(shape):
    zeros = (0,) * len(shape)
    return pl.BlockSpec(shape, lambda *_: zeros)


def _proj_in_kernel(x_ref, g_ref, w_ref, wf_ref, bf_ref, gmg_ref,
                    h_ref, q_ref, k_ref, v_ref, kb_ref, vb_ref, lf_ref, u_ref, vn_ref, mq_ref):
    hb = _rmsnorm(x_ref[...], g_ref[...]).astype(BF16)
    h_ref[...] = hb

    def proj(c):
        return jnp.dot(hb, w_ref[:, c * 512:(c + 1) * 512], preferred_element_type=F32)

    q_ref[...] = (proj(0) * (FOX_HEAD_DIM ** -0.5 * LOG2E)).astype(BF16)
    k = proj(1)
    k_ref[...] = k
    kb_ref[...] = k.astype(BF16)
    v = proj(2)
    v_ref[...] = v
    vb_ref[...] = v.astype(BF16)
    u_ref[...] = _gelu(proj(3))
    vf = _gelu(proj(4))
    mu = jnp.mean(vf, axis=-1, keepdims=True)
    var = jnp.mean(jnp.square(vf - mu), axis=-1, keepdims=True)
    vn_ref[...] = (vf - mu) * lax.rsqrt(var + EPS) * gmg_ref[...]
    mq_ref[...] = proj(5)
    f = jnp.dot(hb, wf_ref[...], preferred_element_type=F32) + bf_ref[...]
    lf_ref[...] = jnp.minimum(f, 0.0) - jnp.log1p(jnp.exp(-jnp.abs(f)))


def _proj_in(x, g, w_main, w_f, b_f, gm_g):
    n, d = x.shape
    tm = TOKEN_TILE
    row = lambda width: pl.BlockSpec((tm, width), lambda i: (i, 0))
    out_shapes = (
        jax.ShapeDtypeStruct((n, d), BF16),
        jax.ShapeDtypeStruct((n, FOX_WIDTH), BF16),
        jax.ShapeDtypeStruct((n, FOX_WIDTH), F32),
        jax.ShapeDtypeStruct((n, FOX_WIDTH), F32),
        jax.ShapeDtypeStruct((n, FOX_WIDTH), BF16),
        jax.ShapeDtypeStruct((n, FOX_WIDTH), BF16),
        jax.ShapeDtypeStruct((n, LANES), F32),
        jax.ShapeDtypeStruct((n, GM_WIDTH), F32),
        jax.ShapeDtypeStruct((n, GM_WIDTH), F32),
        jax.ShapeDtypeStruct((n, MEM_WIDTH), F32),
    )
    return pl.pallas_call(
        _proj_in_kernel,
        grid=(n // tm,),
        in_specs=[row(d), _const_spec((1, d)), _const_spec(w_main.shape), _const_spec(w_f.shape),
                  _const_spec((1, LANES)), _const_spec((1, GM_WIDTH))],
        out_specs=(row(d), row(FOX_WIDTH), row(FOX_WIDTH), row(FOX_WIDTH), row(FOX_WIDTH),
                   row(FOX_WIDTH), row(LANES), row(GM_WIDTH), row(GM_WIDTH), row(MEM_WIDTH)),
        out_shape=out_shapes,
        compiler_params=_params("parallel"),
        name="proj_in",
    )(x, g, w_main, w_f, b_f, gm_g)


def _cumsum_kernel(lf_ref, o_ref, parts_ref, carry_ref):
    @pl.when(pl.program_id(1) == 0)
    def _():
        carry_ref[...] = jnp.zeros_like(carry_ref)

    tc = lf_ref.shape[-1]
    upper = (lax.broadcasted_iota(jnp.int32, (tc, tc), 0)
             <= lax.broadcasted_iota(jnp.int32, (tc, tc), 1)).astype(F32)
    y = jnp.dot(lf_ref[...], upper, precision=HIGHEST, preferred_element_type=F32) + carry_ref[...]
    carry_ref[...] = y[:, tc - 1:tc]
    y = y * LOG2E
    o_ref[...] = y
    hi = y.astype(BF16).astype(F32)
    mid = (y - hi).astype(BF16).astype(F32)
    parts_ref[0] = hi
    parts_ref[1] = mid
    parts_ref[2] = ((y - hi) - mid).astype(BF16).astype(F32)


def _cumsum_lanes(lf_t):
    b, h, s = lf_t.shape
    tc = min(CUMSUM_TILE, s)
    return pl.pallas_call(
        _cumsum_kernel,
        grid=(b, s // tc),
        in_specs=[pl.BlockSpec((None, h, tc), lambda i, j: (i, 0, j))],
        out_specs=(pl.BlockSpec((None, h, tc), lambda i, j: (i, 0, j)),
                   pl.BlockSpec((None, 3, h, tc), lambda i, j: (i, 0, 0, j))),
        out_shape=(jax.ShapeDtypeStruct((b, h, s), F32), jax.ShapeDtypeStruct((b, 3, h, s), F32)),
        scratch_shapes=[pltpu.VMEM((h, 1), F32)],
        compiler_params=_params("parallel", "arbitrary"),
        name="fox_cumsum",
    )(lf_t)


def _fox_attn_kernel(qi_tab, kj_tab, q_ref, k_ref, fa_ref, v_ref, fr_ref, o_ref,
                     qm_ref, fr_rep, m_ref, l_ref, alpha_ref, shift_ref, tmax_ref, acc_ref, s_ref, p_ref,
                     *, tq, tk):
    p = pl.program_id(2)
    qi = qi_tab[p]
    kj = kj_tab[p]
    lane = lax.broadcasted_iota(jnp.int32, (tq, LANES), 1)
    low = lane < FOX_HEAD_DIM
    n_col = tk // LANES
    n_chunk = tq // ATTN_ROWS

    @pl.when(kj == 0)
    def _():
        q = q_ref[...]
        for hh in range(2):
            qm_ref[hh, :, :LANES] = jnp.where(low if hh == 0 else jnp.logical_not(low), q, jnp.zeros_like(q))
            minus_one = (lane >= 3 * hh) & (lane < 3 * hh + 3)
            qm_ref[hh, :, LANES:] = jnp.where(minus_one, -1.0, 0.0).astype(BF16)
            fr_rep[hh] = jnp.broadcast_to(fr_ref[:, hh:hh + 1], (tq, LANES))
        m_ref[...] = jnp.full_like(m_ref, -jnp.inf)
        l_ref[...] = jnp.zeros_like(l_ref)
        acc_ref[...] = jnp.zeros_like(acc_ref)

    def step(masked):
        sub = lax.broadcasted_iota(jnp.int32, (ATTN_ROWS, LANES), 0)
        lane_r = lax.broadcasted_iota(jnp.int32, (ATTN_ROWS, LANES), 1)
        ka = jnp.concatenate([k_ref[...], fa_ref[...]], axis=1)
        for hh in range(2):
            s_ref[hh] = lax.dot_general(qm_ref[hh], ka, (((1,), (1,)), ((), ())), preferred_element_type=F32)

        def scores(hh, r0):
            ts = []
            for j in range(n_col):
                t = s_ref[hh, pl.ds(r0, ATTN_ROWS), j * LANES:(j + 1) * LANES]
                if masked:
                    keep = (kj * tk + j * LANES + lane_r) <= (qi * tq + r0 + sub)
                    t = jnp.where(keep, t, -jnp.inf)
                ts.append(t)
            return ts

        for hh in range(2):
            def max_body(i, carry, hh=hh):
                r0 = pl.multiple_of(i * ATTN_ROWS, ATTN_ROWS)
                tmax_ref[hh, pl.ds(r0, ATTN_ROWS), :] = functools.reduce(jnp.maximum, scores(hh, r0))
                return carry

            lax.fori_loop(0, n_chunk, max_body, 0, unroll=ATTN_MAX_UNROLL)

        for hh in range(2):
            f_row = fr_rep[hh]
            m_prev = m_ref[hh]
            m_new = jnp.maximum(m_prev, jnp.max(tmax_ref[hh], axis=1, keepdims=True) + f_row)
            m_ref[hh] = m_new
            alpha_ref[hh] = jnp.exp2(m_prev - m_new)
            shift_ref[hh] = m_new - f_row

        v2 = v_ref[...]
        pvs = []
        for hh in range(2):
            def exp_body(i, carry, hh=hh):
                r0 = pl.multiple_of(i * ATTN_ROWS, ATTN_ROWS)
                rows = pl.ds(r0, ATTN_ROWS)
                shift = shift_ref[hh, rows, :]
                prs = [jnp.exp2(t - shift) for t in scores(hh, r0)]
                l_ref[hh, rows, :] = alpha_ref[hh, rows, :] * l_ref[hh, rows, :] + functools.reduce(jnp.add, prs)
                for j in range(n_col):
                    p_ref[hh, rows, j * LANES:(j + 1) * LANES] = prs[j].astype(BF16)
                return carry

            lax.fori_loop(0, n_chunk, exp_body, 0, unroll=ATTN_EXP_UNROLL)
            pvs.append(jnp.dot(p_ref[hh], v2, preferred_element_type=F32))
        acc = acc_ref[...]
        acc_ref[...] = jnp.where(low, alpha_ref[0] * acc + pvs[0], alpha_ref[1] * acc + pvs[1])

    on_diagonal = (kj + 1) * tk > qi * tq + 1
    pl.when(on_diagonal)(lambda: step(True))
    pl.when(jnp.logical_not(on_diagonal))(lambda: step(False))

    @pl.when((kj + 1) * tk >= (qi + 1) * tq)
    def _():
        l0 = jnp.sum(l_ref[0], axis=1, keepdims=True)
        l1 = jnp.sum(l_ref[1], axis=1, keepdims=True)
        o_ref[...] = (acc_ref[...] * jnp.where(low, 1.0 / l0, 1.0 / l1)).astype(BF16)


def _fox_prompt(q, kb, f_aug, vb, f_row, batch, seq):
    tq = min(ATTN_TQ, seq)
    tk = min(ATTN_TK, tq)
    nq = seq // tq
    pairs = [(i, j) for i in range(nq) for j in range(((i + 1) * tq) // tk)]
    qi_tab = jnp.asarray(np.array([a for a, _ in pairs], np.int32))
    kj_tab = jnp.asarray(np.array([c for _, c in pairs], np.int32))
    nqb, nkb = seq // tq, seq // tk
    grid_spec = pltpu.PrefetchScalarGridSpec(
        num_scalar_prefetch=2,
        grid=(batch, FOX_HEADS // 2, len(pairs)),
        in_specs=[
            pl.BlockSpec((tq, LANES), lambda b, h, p, qt, kt: (b * nqb + qt[p], h)),
            pl.BlockSpec((tk, LANES), lambda b, h, p, qt, kt: (b * nkb + kt[p], h)),
            pl.BlockSpec((tk, LANES), lambda b, h, p, qt, kt: (b * nkb + kt[p], h)),
            pl.BlockSpec((tk, LANES), lambda b, h, p, qt, kt: (b * nkb + kt[p], h)),
            pl.BlockSpec((None, None, tq, 2), lambda b, h, p, qt, kt: (b, h, qt[p], 0)),
        ],
        out_specs=pl.BlockSpec((tq, LANES), lambda b, h, p, qt, kt: (b * nqb + qt[p], h)),
        scratch_shapes=[pltpu.VMEM((2, tq, 2 * LANES), BF16),
                        pltpu.VMEM((2, tq, LANES), F32),
                        pltpu.VMEM((2, tq, LANES), F32),
                        pltpu.VMEM((2, tq, LANES), F32),
                        pltpu.VMEM((2, tq, LANES), F32),
                        pltpu.VMEM((2, tq, LANES), F32),
                        pltpu.VMEM((2, tq, LANES), F32),
                        pltpu.VMEM((tq, LANES), F32),
                        pltpu.VMEM((2, tq, tk), F32),
                        pltpu.VMEM((2, tq, tk), BF16)],
    )
    return pl.pallas_call(
        functools.partial(_fox_attn_kernel, tq=tq, tk=tk),
        grid_spec=grid_spec,
        out_shape=jax.ShapeDtypeStruct((batch * seq, FOX_WIDTH), BF16),
        compiler_params=_params("parallel", "parallel", "arbitrary"),
        name="fox_prompt_attn",
    )(qi_tab, kj_tab, q, kb, f_aug, vb, f_row)


def _fox_sample_kernel(pt_ref, q_ref, kn_ref, vn_ref, lfn_ref, *refs, n_pages, t_new):
    k_refs = refs[:n_pages]
    v_refs = refs[n_pages:2 * n_pages]
    lf_refs = refs[2 * n_pages:3 * n_pages]
    o_ref = refs[3 * n_pages]
    t_ref = refs[3 * n_pages + 1]
    rows = FOX_HEADS * t_new
    nt = (((1,), (1,)), ((), ()))

    q = q_ref[...]
    qt = jnp.concatenate([q] * FOX_HEADS, axis=0)
    r_w = lax.broadcasted_iota(jnp.int32, (rows, FOX_WIDTH), 0)
    c_w = lax.broadcasted_iota(jnp.int32, (rows, FOX_WIDTH), 1)
    head_match = _div(r_w, t_new) == _div(c_w, FOX_HEAD_DIM)
    q_bd = jnp.where(head_match, qt, 0.0).astype(BF16)
    upper = (lax.broadcasted_iota(jnp.int32, (PAGE_SIZE, PAGE_SIZE), 0)
             <= lax.broadcasted_iota(jnp.int32, (PAGE_SIZE, PAGE_SIZE), 1)).astype(F32)
    r_p = lax.broadcasted_iota(jnp.int32, (rows, PAGE_SIZE), 0)
    c_p = lax.broadcasted_iota(jnp.int32, (rows, PAGE_SIZE), 1)
    pad = jnp.zeros((PAGE_SIZE - t_new, FOX_WIDTH), F32)

    carry = jnp.zeros((FOX_HEADS, 1), F32)
    m = jnp.full((rows, 1), -jnp.inf, F32)
    f_row = None
    for pg in range(n_pages + 1):
        if pg < n_pages:
            k_t = k_refs[pg][...].reshape(FOX_WIDTH, PAGE_SIZE).astype(BF16)
            s = jnp.dot(q_bd, k_t, preferred_element_type=F32)
            lf_t = lf_refs[pg][...]
        else:
            k_new = jnp.concatenate([kn_ref[...], pad], axis=0).astype(BF16)
            s = lax.dot_general(q_bd, k_new, nt, preferred_element_type=F32)
            lf_t = lfn_ref[...]
        f_t = jnp.dot(lf_t, upper, precision=HIGHEST, preferred_element_type=F32) + carry
        carry = f_t[:, PAGE_SIZE - 1:PAGE_SIZE]
        f_t = f_t * LOG2E
        f_col = jnp.concatenate(
            [jnp.broadcast_to(f_t[h:h + 1, :], (t_new, PAGE_SIZE)) for h in range(FOX_HEADS)], axis=0)
        t = s - f_col
        if pg == n_pages:
            t = jnp.where(c_p <= _mod(r_p, t_new), t, -jnp.inf)
            f_row = jnp.sum(jnp.where(c_p == _mod(r_p, t_new), f_col, 0.0), axis=1, keepdims=True)
        t_ref[pg] = t
        m = jnp.maximum(m, jnp.max(t, axis=1, keepdims=True))
    m_logit = m + f_row
    shift = m_logit - f_row
    l = jnp.zeros((rows, 1), F32)
    acc = jnp.zeros((rows, FOX_WIDTH), F32)
    for pg in range(n_pages + 1):
        pr = jnp.exp2(t_ref[pg] - shift)
        l = l + jnp.sum(pr, axis=1, keepdims=True)
        if pg < n_pages:
            v_t = v_refs[pg][...].reshape(FOX_WIDTH, PAGE_SIZE).astype(BF16)
            acc = acc + lax.dot_general(pr.astype(BF16), v_t, nt, preferred_element_type=F32)
        else:
            v_new = jnp.concatenate([vn_ref[...], pad], axis=0).astype(BF16)
            acc = acc + jnp.dot(pr.astype(BF16), v_new, preferred_element_type=F32)
    o = acc / l
    c_o = lax.broadcasted_iota(jnp.int32, (t_new, FOX_WIDTH), 1)
    out = jnp.zeros((t_new, FOX_WIDTH), F32)
    for h in range(FOX_HEADS):
        out = jnp.where(_div(c_o, FOX_HEAD_DIM) == h, o[h * t_new:(h + 1) * t_new, :], out)
    o_ref[...] = out


def _fox_sample(page_table, q32, k32, v32, lf_new_t, cache_kt, cache_vt, cache_lf_t, layer, row0, t_new):
    nb, n_pages = page_table.shape
    blk0 = row0 // t_new
    tok = pl.BlockSpec((t_new, FOX_WIDTH), lambda b, pt: (blk0 + b, 0))

    def kv_spec(pg):
        return pl.BlockSpec((None, None, FOX_HEADS, FOX_HEAD_DIM, PAGE_SIZE),
                            lambda b, pt: (layer, pt[b, pg], 0, 0, 0))

    def lf_spec(pg):
        return pl.BlockSpec((None, None, FOX_HEADS, PAGE_SIZE), lambda b, pt: (layer, pt[b, pg], 0, 0))

    in_specs = [tok, tok, tok, pl.BlockSpec((None, FOX_HEADS, LANES), lambda b, pt: (b, 0, 0))]
    in_specs += [kv_spec(pg) for pg in range(n_pages)] * 2
    in_specs += [lf_spec(pg) for pg in range(n_pages)]
    grid_spec = pltpu.PrefetchScalarGridSpec(
        num_scalar_prefetch=1,
        grid=(nb,),
        in_specs=in_specs,
        out_specs=pl.BlockSpec((t_new, FOX_WIDTH), lambda b, pt: (b, 0)),
        scratch_shapes=[pltpu.VMEM((n_pages + 1, FOX_HEADS * t_new, PAGE_SIZE), F32)],
    )
    return pl.pallas_call(
        functools.partial(_fox_sample_kernel, n_pages=n_pages, t_new=t_new),
        grid_spec=grid_spec,
        out_shape=jax.ShapeDtypeStruct((nb * t_new, FOX_WIDTH), F32),
        compiler_params=_params("parallel"),
        name="fox_sample_attn",
    )(page_table, q32, k32, v32, lf_new_t, *([cache_kt] * n_pages), *([cache_vt] * n_pages),
      *([cache_lf_t] * n_pages))


def _norm_matmul_kernel(x_ref, g_ref, w_ref, o_ref):
    hb = _rmsnorm(x_ref[...], g_ref[...]).astype(BF16)
    o_ref[...] = jnp.dot(hb, w_ref[...], preferred_element_type=F32)


def _norm_matmul(x, g, w):
    n, d = x.shape
    tm = min(TOKEN_TILE, n)
    return pl.pallas_call(
        _norm_matmul_kernel,
        grid=(n // tm,),
        in_specs=[pl.BlockSpec((tm, d), lambda i: (i, 0)), _const_spec((1, d)), _const_spec(w.shape)],
        out_specs=pl.BlockSpec((tm, w.shape[1]), lambda i: (i, 0)),
        out_shape=jax.ShapeDtypeStruct((n, w.shape[1]), F32),
        compiler_params=_params("parallel"),
        name="mem_kv",
    )(x, g, w)


def _mem_attend_kernel(q_ref, mk_ref, mv_ref, o_ref):
    scale = MEM_HEAD_DIM ** -0.5
    n_mem = mk_ref.shape[0]
    t = q_ref.shape[0] // n_mem
    for g in range(n_mem):
        rs = slice(g * t, (g + 1) * t)
        for h in range(MEM_HEADS):
            sl = slice(h * MEM_HEAD_DIM, (h + 1) * MEM_HEAD_DIM)
            s = lax.dot_general(q_ref[rs, sl].astype(BF16), mk_ref[g, :, sl].astype(BF16),
                                (((1,), (1,)), ((), ())), preferred_element_type=F32) * scale
            e = jnp.exp(s - jnp.max(s, axis=1, keepdims=True))
            pr = e / jnp.sum(e, axis=1, keepdims=True)
            o_ref[rs, sl] = jnp.dot(pr.astype(BF16), mv_ref[g, :, sl].astype(BF16), preferred_element_type=F32)


def _mem_attend(mq, mk, mv, row0, n_mem, t_per_mem, name):
    tm = min(TOKEN_TILE * 2, t_per_mem)
    tiles = t_per_mem // tm
    group = max(1, min(MEM_GROUP_ROWS // tm, n_mem)) if tiles == 1 else 1
    assert n_mem % group == 0
    rows = tm * group
    blk0 = row0 // rows
    m_len = mk.shape[1]
    mem_spec = pl.BlockSpec((group, m_len, MEM_WIDTH), lambda b, i: (b, 0, 0))
    return pl.pallas_call(
        _mem_attend_kernel,
        grid=(n_mem // group, tiles),
        in_specs=[pl.BlockSpec((rows, MEM_WIDTH), lambda b, i: (blk0 + b * tiles + i, 0)), mem_spec, mem_spec],
        out_specs=pl.BlockSpec((rows, MEM_WIDTH), lambda b, i: (b * tiles + i, 0)),
        out_shape=jax.ShapeDtypeStruct((n_mem * t_per_mem, MEM_WIDTH), F32),
        compiler_params=_params("parallel", "parallel"),
        name=name,
    )(mq, mk, mv)


def _merge_kernel(x_ref, h_ref, fox_ref, u_ref, vn_ref, mem_ref, ws_ref, bs_ref,
                  wg_ref, bg_ref, wbr_ref, wo_ref, o_ref, gm_ref, *, chunk):
    tm, d = x_ref.shape
    r_c = lax.broadcasted_iota(jnp.int32, (CHUNK, CHUNK), 0)
    c_c = lax.broadcasted_iota(jnp.int32, (CHUNK, CHUNK), 1)
    keep = (c_c <= r_c) & (_div(r_c, chunk) == _div(c_c, chunk))
    for g in range(GM_GROUPS):
        w = jnp.where(keep, ws_ref[g], 0.0).astype(BF16)
        sl = slice(g * GM_GROUP_DIM, (g + 1) * GM_GROUP_DIM)
        for c in range(tm // CHUNK):
            rs = slice(c * CHUNK, (c + 1) * CHUNK)
            mix = jnp.dot(w, vn_ref[rs, sl].astype(BF16), preferred_element_type=F32) + bs_ref[:, g:g + 1]
            gm_ref[rs, sl] = (u_ref[rs, sl] * mix).astype(BF16)

    hb = h_ref[...]
    branches = (fox_ref[...].astype(BF16), gm_ref[...], mem_ref[...].astype(BF16))
    merged = None
    for r in range(N_BRANCH):
        gate = jax.nn.sigmoid(jnp.dot(hb, wg_ref[:, r * d:(r + 1) * d], preferred_element_type=F32)
                              + bg_ref[:, r * d:(r + 1) * d])
        term = gate * jnp.dot(branches[r], wbr_ref[r], preferred_element_type=F32)
        merged = term if merged is None else merged + term
    o_ref[...] = x_ref[...] + jnp.dot(merged.astype(BF16), wo_ref[...], preferred_element_type=F32)


def _merge(x, h, fox_o, u, vn, mem_o, ws, bs_t, w_gate, b_gate, w_br, w_o, tile0, ntiles, chunk, name):
    d = x.shape[1]
    tm = TOKEN_TILE
    row = lambda width: pl.BlockSpec((tm, width), lambda i: (tile0 + i, 0))
    own = lambda width: pl.BlockSpec((tm, width), lambda i: (i, 0))
    in_specs = [row(d), row(d), own(FOX_WIDTH), row(GM_WIDTH), row(GM_WIDTH), own(MEM_WIDTH),
                _const_spec(ws.shape), _const_spec(bs_t.shape),
                _const_spec(w_gate.shape), _const_spec((1, N_BRANCH * d)), _const_spec(w_br.shape),
                _const_spec(w_o.shape)]
    return pl.pallas_call(
        functools.partial(_merge_kernel, chunk=chunk),
        grid=(ntiles,),
        in_specs=in_specs,
        out_specs=pl.BlockSpec((tm, d), lambda i: (i, 0)),
        out_shape=jax.ShapeDtypeStruct((ntiles * tm, d), F32),
        scratch_shapes=[pltpu.VMEM((tm, GM_WIDTH), BF16)],
        compiler_params=_params("parallel"),
        name=name,
    )(x, h, fox_o, u, vn, mem_o, ws, bs_t, w_gate, b_gate, w_br, w_o)


def _moe_route_kernel(x_ref, g_ref, wr_ref, br_ref, hn_ref, meta_ref, cnt_ref, carry_ref):
    @pl.when(pl.program_id(0) == 0)
    def _():
        carry_ref[...] = jnp.zeros_like(carry_ref)

    tm = x_ref.shape[0]
    hn = _rmsnorm(x_ref[...], g_ref[...])
    hn_ref[...] = hn
    logits = jnp.dot(hn, wr_ref[...], precision=HIGHEST, preferred_element_type=F32) + br_ref[...]
    col = lax.broadcasted_iota(jnp.int32, (tm, LANES), 1)
    colf = col.astype(F32)
    big = jnp.float32(1e9)

    def masked_softmax(mask):
        z = jnp.where(mask, logits, -jnp.inf)
        e = jnp.exp(z - jnp.max(z, axis=1, keepdims=True))
        return e / jnp.sum(e, axis=1, keepdims=True)

    def first_argmax(vals, vmax, mask):
        return jnp.min(jnp.where(mask & (vals == vmax), colf, big), axis=1, keepdims=True)

    is_group = col < N_GROUPS
    g_prob = masked_softmax(is_group)
    p_g = jnp.max(g_prob, axis=1, keepdims=True)
    g_idx = first_argmax(g_prob, p_g, is_group)
    e_col = col - N_GROUPS
    in_group = (e_col >= 0) & (e_col < N_EXPERTS) & (_div(e_col, EXPERTS_PER_GROUP).astype(F32) == g_idx)
    e_prob = masked_softmax(in_group)
    p1 = jnp.max(e_prob, axis=1, keepdims=True)
    i1 = first_argmax(e_prob, p1, in_group)
    rest = in_group & (colf != i1)
    p2 = jnp.max(jnp.where(rest, e_prob, -1.0), axis=1, keepdims=True)
    i2 = first_argmax(e_prob, p2, rest)
    denom = p1 + p2
    w1 = p1 / denom * p_g
    w2 = p2 / denom * p_g
    e1 = i1 - N_GROUPS
    e2 = i2 - N_GROUPS

    onehot = ((colf == e1) | (colf == e2)).astype(F32)
    lower = (lax.broadcasted_iota(jnp.int32, (tm, tm), 1)
             < lax.broadcasted_iota(jnp.int32, (tm, tm), 0)).astype(BF16)
    before = jnp.dot(lower, onehot.astype(BF16), preferred_element_type=F32) + carry_ref[...]
    r1 = jnp.sum(jnp.where(colf == e1, before, 0.0), axis=1, keepdims=True)
    r2 = jnp.sum(jnp.where(colf == e2, before, 0.0), axis=1, keepdims=True)
    carry_ref[...] = carry_ref[...] + jnp.sum(onehot, axis=0, keepdims=True)
    cnt_ref[...] = jnp.broadcast_to(carry_ref[...], cnt_ref.shape)

    meta = jnp.zeros((tm, LANES), F32)
    for lane_id, val in enumerate((e1, e2, r1, r2, w1, w2)):
        meta = jnp.where(col == lane_id, val, meta)
    meta_ref[...] = meta


def _moe_route(x, g, w_route, b_route):
    n, d = x.shape
    tm = ROUTE_TILE
    return pl.pallas_call(
        _moe_route_kernel,
        grid=(n // tm,),
        in_specs=[pl.BlockSpec((tm, d), lambda i: (i, 0)), _const_spec((1, d)),
                  _const_spec(w_route.shape), _const_spec((1, LANES))],
        out_specs=(pl.BlockSpec((tm, d), lambda i: (i, 0)), pl.BlockSpec((tm, LANES), lambda i: (i, 0)),
                   _const_spec((8, LANES))),
        out_shape=(jax.ShapeDtypeStruct((n, d), F32), jax.ShapeDtypeStruct((n, LANES), F32),
                   jax.ShapeDtypeStruct((8, LANES), F32)),
        scratch_shapes=[pltpu.VMEM((1, LANES), F32)],
        compiler_params=_params("arbitrary"),
        name="moe_route",
    )(x, g, w_route, b_route)


def _moe_expert_kernel(te_ref, tv_ref, src_ref, src_next_ref, hn_hbm, wgu_ref, wdn_ref, ys_ref,
                       x_buf, wgu_bf, wdn_bf, sem, *, tg):
    t = pl.program_id(0)
    n_t = pl.num_programs(0)
    slot = t % 2

    def gather(idx_ref, buf_slot):
        def issue(r, c):
            pltpu.make_async_copy(hn_hbm.at[pl.ds(idx_ref[0, r], 1)], x_buf.at[buf_slot, pl.ds(r, 1)],
                                  sem.at[buf_slot]).start()
            return c

        lax.fori_loop(0, tg, issue, 0, unroll=DMA_UNROLL)

    @pl.when((t == 0) & (tv_ref[0] == 1))
    def _():
        gather(src_ref, 0)

    @pl.when((t + 1 < n_t) & (tv_ref[jnp.minimum(t + 1, n_t - 1)] == 1))
    def _():
        gather(src_next_ref, 1 - slot)

    @pl.when(tv_ref[t] == 1)
    def _():
        @pl.when((t == 0) | (te_ref[t] != te_ref[jnp.maximum(t - 1, 0)]))
        def _():
            wgu_bf[...] = wgu_ref[...].astype(BF16)
            wdn_bf[...] = wdn_ref[...].astype(BF16)

        pltpu.make_async_copy(hn_hbm.at[pl.ds(0, tg)], x_buf.at[slot], sem.at[slot]).wait()
        gu = jnp.dot(x_buf[slot].astype(BF16), wgu_bf[...], preferred_element_type=F32)
        gate = gu[:, :EXPERT_FF]
        act = gate * jax.nn.sigmoid(gate) * gu[:, EXPERT_FF:]
        ys_ref[...] = jnp.dot(act.astype(BF16), wdn_bf[...], preferred_element_type=F32)

    @pl.when(tv_ref[t] != 1)
    def _():
        ys_ref[...] = jnp.zeros_like(ys_ref)


def _moe_experts(tile_expert, tile_valid, src_idx, hn, w_gu, w_dn, layer):
    n_tiles = tile_expert.shape[0]
    tg = MOE_TILE
    d = hn.shape[1]
    grid_spec = pltpu.PrefetchScalarGridSpec(
        num_scalar_prefetch=2,
        grid=(n_tiles,),
        in_specs=[
            pl.BlockSpec((None, 1, tg), lambda t, te, tv: (t, 0, 0), memory_space=pltpu.SMEM),
            pl.BlockSpec((None, 1, tg), lambda t, te, tv: (jnp.minimum(t + 1, n_tiles - 1), 0, 0),
                         memory_space=pltpu.SMEM),
            pl.BlockSpec(memory_space=pl.ANY),
            pl.BlockSpec((None, None, d, 2 * EXPERT_FF), lambda t, te, tv: (layer, te[t], 0, 0)),
            pl.BlockSpec((None, None, EXPERT_FF, d), lambda t, te, tv: (layer, te[t], 0, 0)),
        ],
        out_specs=pl.BlockSpec((tg, d), lambda t, te, tv: (t, 0)),
        scratch_shapes=[pltpu.VMEM((2, tg, d), F32), pltpu.VMEM((d, 2 * EXPERT_FF), BF16),
                        pltpu.VMEM((EXPERT_FF, d), BF16), pltpu.SemaphoreType.DMA((2,))],
    )
    return pl.pallas_call(
        functools.partial(_moe_expert_kernel, tg=tg),
        grid_spec=grid_spec,
        out_shape=jax.ShapeDtypeStruct((n_tiles * tg, d), F32),
        compiler_params=_params("arbitrary"),
        name="moe_experts",
    )(tile_expert, tile_valid, src_idx, src_idx, hn, w_gu, w_dn)


def _moe_combine_kernel(dest_ref, x_ref, meta_ref, fg_ref, ys_hbm, o_ref, buf, sem, *, tm, final):
    def issue(r, c):
        for slot in range(2):
            pltpu.make_async_copy(ys_hbm.at[pl.ds(dest_ref[0, 2 * r + slot], 1)],
                                  buf.at[slot, pl.ds(r, 1)], sem.at[slot]).start()
        return c

    lax.fori_loop(0, tm, issue, 0, unroll=DMA_UNROLL)
    for slot in range(2):
        pltpu.make_async_copy(ys_hbm.at[pl.ds(0, tm)], buf.at[slot], sem.at[slot]).wait()
    meta = meta_ref[...]
    y = x_ref[...] + (meta[:, 4:5] * buf[0] + meta[:, 5:6] * buf[1])
    if final:
        y = _rmsnorm(y, fg_ref[...])
    o_ref[...] = y


def _moe_combine(dest, x, meta, final_g, ys, final):
    n, d = x.shape
    tm = TOKEN_TILE
    return pl.pallas_call(
        functools.partial(_moe_combine_kernel, tm=tm, final=final),
        grid=(n // tm,),
        in_specs=[pl.BlockSpec((None, 1, 2 * tm), lambda i: (i, 0, 0), memory_space=pltpu.SMEM),
                  pl.BlockSpec((tm, d), lambda i: (i, 0)), pl.BlockSpec((tm, LANES), lambda i: (i, 0)),
                  _const_spec((1, d)), pl.BlockSpec(memory_space=pl.ANY)],
        out_specs=pl.BlockSpec((tm, d), lambda i: (i, 0)),
        out_shape=jax.ShapeDtypeStruct((n, d), F32),
        scratch_shapes=[pltpu.VMEM((2, tm, d), F32), pltpu.SemaphoreType.DMA((2,))],
        compiler_params=_params("arbitrary"),
        name="moe_combine",
    )(dest, x, meta, final_g, ys)


def _moe_layer(x, g, w_route, b_route, w_gu, w_dn, layer, final_g, final):
    n, d = x.shape
    tg = MOE_TILE
    hn, meta, counts = _moe_route(x, g, w_route, b_route)
    counts = counts[0, :N_EXPERTS].astype(jnp.int32)
    padded = ((counts + tg - 1) // tg) * tg
    ends = jnp.cumsum(padded)
    starts = ends - padded
    eid = meta[:, 0:2].astype(jnp.int32)
    rank = meta[:, 2:4].astype(jnp.int32)
    dest = starts[eid] + rank
    n_tiles = (2 * n) // tg + N_EXPERTS
    tile_start = jnp.arange(n_tiles, dtype=jnp.int32) * tg
    tile_expert = jnp.minimum(jnp.sum((ends[None, :] <= tile_start[:, None]).astype(jnp.int32), axis=1),
                              N_EXPERTS - 1)
    tile_valid = (tile_start < ends[-1]).astype(jnp.int32)
    token = jnp.broadcast_to(jnp.arange(n, dtype=jnp.int32)[:, None], (n, 2))
    src_idx = jnp.zeros((n_tiles * tg,), jnp.int32).at[dest.reshape(-1)].set(token.reshape(-1))
    ys = _moe_experts(tile_expert, tile_valid, src_idx.reshape(n_tiles, 1, tg), hn, w_gu, w_dn, layer)
    return _moe_combine(dest.reshape(n // TOKEN_TILE, 1, 2 * TOKEN_TILE), x, meta, final_g, ys, final)


def kernel(x_prompt, x_sample, cache_k, cache_v, cache_logf, cache_mem_k, cache_mem_v, page_table, mem_prompt, norm1_g, w_in, b_f, gm_norm_g, gm_ws, gm_bs, mem_norm_g, w_mem_kv, w_gate, b_gate, w_br, w_o, norm2_g, w_rg, b_rg, w_re, b_re, w_moe_gu, w_moe_dn, final_g):
    batch, seq, d = x_prompt.shape
    dec_batch, dec_seq, _ = x_sample.shape
    depth = w_in.shape[0]
    n_pool = cache_k.shape[1]
    mem_len = mem_prompt.shape[1]
    n_p = batch * seq
    n_s = dec_batch * dec_seq
    assert n_p % (2 * TOKEN_TILE) == 0 and n_s % (2 * TOKEN_TILE) == 0 and seq % CHUNK == 0
    assert CHUNK % dec_seq == 0 and (n_p + n_s) % ROUTE_TILE == 0
    n_prompt_tiles = n_p // TOKEN_TILE

    x = jnp.concatenate([x_prompt.reshape(n_p, d), x_sample.reshape(n_s, d)], axis=0)
    mem_flat = mem_prompt.reshape(batch * mem_len, d)
    ckt = jnp.transpose(cache_k, (0, 1, 3, 4, 2))
    cvt = jnp.transpose(cache_v, (0, 1, 3, 4, 2))
    clf_t = jnp.swapaxes(cache_logf, 2, 3)

    q_end, f_end = 3 * FOX_WIDTH, 3 * FOX_WIDTH + FOX_HEADS
    reps = CHUNK // dec_seq
    outs = {name: [] for name in ("kp", "vp", "fp", "mkp", "mvp", "ks", "vs", "fs", "gs")}
    for l in range(depth):
        w_main = jnp.concatenate([w_in[l][:, :q_end], w_in[l][:, f_end:]], axis=1).astype(BF16)
        w_f = jnp.pad(w_in[l][:, q_end:f_end], ((0, 0), (0, LANES - FOX_HEADS))).astype(BF16)
        bf_pad = jnp.pad(b_f[l], (0, LANES - FOX_HEADS)).reshape(1, LANES)
        ws_sample = jnp.tile(gm_ws[l][:, :dec_seq, :dec_seq], (1, reps, reps))
        bs_sample_t = jnp.tile(gm_bs[l][:, :dec_seq], (1, reps)).T
        w_route = jnp.pad(jnp.concatenate([w_rg[l], w_re[l]], axis=1),
                          ((0, 0), (0, LANES - N_GROUPS - N_EXPERTS)))
        b_route = jnp.pad(jnp.concatenate([b_rg[l], b_re[l]]), (0, LANES - N_GROUPS - N_EXPERTS)).reshape(1, LANES)

        h, q, k32, v32, kb, vb, lf, u, vn, mq = _proj_in(
            x, norm1_g[l].reshape(1, d), w_main, w_f, bf_pad, gm_norm_g[l].reshape(1, GM_WIDTH))

        lf_p = lf[:n_p, :FOX_HEADS].reshape(batch, seq, FOX_HEADS)
        f_t, f_parts = _cumsum_lanes(jnp.swapaxes(lf_p, 1, 2))
        f_row = jnp.swapaxes(f_t.reshape(batch, FOX_HEADS // 2, 2, seq), 2, 3)
        f_aug = jnp.transpose(f_parts, (0, 3, 2, 1)).reshape(batch, seq, FOX_HEADS // 2, 6)
        f_aug = jnp.pad(f_aug, ((0, 0), (0, 0), (0, 0), (0, LANES - 6))).astype(BF16).reshape(n_p, 4 * LANES)
        fox_p = _fox_prompt(q, kb, f_aug, vb, f_row, batch, seq)

        lf_s = lf[n_p:, :FOX_HEADS].reshape(dec_batch, dec_seq, FOX_HEADS)
        lf_new_t = jnp.pad(jnp.swapaxes(lf_s, 1, 2), ((0, 0), (0, 0), (0, LANES - dec_seq)))
        fox_s = _fox_sample(page_table, q.astype(F32), k32, v32, lf_new_t, ckt, cvt, clf_t, l, n_p, dec_seq)

        kv = _norm_matmul(mem_flat, mem_norm_g[l].reshape(1, d), w_mem_kv[l].astype(BF16))
        mk_p = kv[:, :MEM_WIDTH].reshape(batch, mem_len, MEM_WIDTH)
        mv_p = kv[:, MEM_WIDTH:].reshape(batch, mem_len, MEM_WIDTH)
        mem_p = _mem_attend(mq, mk_p, mv_p, 0, batch, seq, "mem_attend_prompt")
        mem_s = _mem_attend(mq, cache_mem_k[l].reshape(dec_batch, mem_len, MEM_WIDTH),
                            cache_mem_v[l].reshape(dec_batch, mem_len, MEM_WIDTH), n_p, dec_batch, dec_seq,
                            "mem_attend_sample")

        merge_w = (w_gate[l].astype(BF16), b_gate[l].reshape(1, N_BRANCH * d), w_br[l].astype(BF16),
                   w_o[l].astype(BF16))
        x_p = _merge(x, h, fox_p, u, vn, mem_p, gm_ws[l], gm_bs[l].T, *merge_w,
                     0, n_prompt_tiles, CHUNK, "merge_prompt")
        x_s = _merge(x, h, fox_s, u, vn, mem_s, ws_sample, bs_sample_t, *merge_w,
                     n_prompt_tiles, n_s // TOKEN_TILE, dec_seq, "merge_sample")
        x = jnp.concatenate([x_p, x_s], axis=0)
        x = _moe_layer(x, norm2_g[l].reshape(1, d), w_route, b_route, w_moe_gu, w_moe_dn, l,
                       final_g.reshape(1, d), l == depth - 1)

        outs["kp"].append(k32[:n_p].reshape(batch, seq, FOX_HEADS, FOX_HEAD_DIM))
        outs["vp"].append(v32[:n_p].reshape(batch, seq, FOX_HEADS, FOX_HEAD_DIM))
        outs["fp"].append(lf_p)
        outs["mkp"].append(mk_p.reshape(batch, mem_len, MEM_HEADS, MEM_HEAD_DIM))
        outs["mvp"].append(mv_p.reshape(batch, mem_len, MEM_HEADS, MEM_HEAD_DIM))
        outs["ks"].append(k32[n_p:].reshape(dec_batch, dec_seq, FOX_HEADS, FOX_HEAD_DIM))
        outs["vs"].append(v32[n_p:].reshape(dec_batch, dec_seq, FOX_HEADS, FOX_HEAD_DIM))
        outs["fs"].append(lf_s)
        outs["gs"].append(vn[n_p:].reshape(dec_batch, dec_seq, GM_WIDTH))

    y_prompt = x[:n_p].reshape(batch, seq, d)
    y_sample = x[n_p:].reshape(dec_batch, dec_seq, d)
    return (y_prompt, y_sample) + tuple(
        jnp.stack(outs[name]) for name in ("kp", "vp", "fp", "mkp", "mvp", "ks", "vs", "fs", "gs"))
```

```python
import functools

import numpy as np
import jax
import jax.numpy as jnp
from jax import lax
from jax.experimental import pallas as pl
from jax.experimental.pallas import tpu as pltpu

F32 = jnp.float32
BF16 = jnp.bfloat16
HIGHEST = lax.Precision.HIGHEST

EPS = 1e-6
LOG2E = 1.4426950408889634
FOX_HEADS = 8
FOX_HEAD_DIM = 64
FOX_WIDTH = FOX_HEADS * FOX_HEAD_DIM
GM_GROUPS = 4
GM_GROUP_DIM = 128
GM_WIDTH = GM_GROUPS * GM_GROUP_DIM
CHUNK = 128
MEM_HEADS = 4
MEM_HEAD_DIM = 128
MEM_WIDTH = MEM_HEADS * MEM_HEAD_DIM
N_BRANCH = 3
N_GROUPS = 4
EXPERTS_PER_GROUP = 8
N_EXPERTS = N_GROUPS * EXPERTS_PER_GROUP
EXPERT_FF = 512
PAGE_SIZE = 128

LANES = 128
VMEM_LIMIT_BYTES = 56 * 1024 * 1024
TOKEN_TILE = 256
ATTN_TQ = 1024
ATTN_TK = 512
ATTN_ROWS = 16
ATTN_MAX_UNROLL = 64
ATTN_EXP_UNROLL = 64
CUMSUM_TILE = 512
MOE_TILE = 256
ROUTE_TILE = 512
MEM_GROUP_ROWS = 64
DMA_UNROLL = 8


def _params(*sem):
    return pltpu.CompilerParams(dimension_semantics=sem, vmem_limit_bytes=VMEM_LIMIT_BYTES)


def _rmsnorm(x, g):
    return x * lax.rsqrt(jnp.mean(x * x, axis=-1, keepdims=True) + EPS) * g


def _gelu(x):
    return 0.5 * x * (1.0 + jnp.tanh(np.sqrt(2.0 / np.pi).astype(np.float32) * (x + 0.044715 * (x * x * x))))


def _div(x, n):
    assert n & (n - 1) == 0
    return lax.shift_right_arithmetic(x, jnp.int32(n.bit_length() - 1))


def _mod(x, n):
    assert n & (n - 1) == 0
    return x & jnp.int32(n - 1)


def _const_spec(shape):
    zeros = (0,) * len(shape)
    return pl.BlockSpec(shape, lambda *_: zeros)


def _log_sigmoid(f):
    return jnp.minimum(f, 0.0) - jnp.log1p(jnp.exp(-jnp.abs(f)))


def _proj_in_kernel(x_ref, g_ref, w_ref, wf_ref, bf_ref, gmg_ref, *refs, positions_minor):
    if positions_minor:
        h_ref, q_ref, kbt_ref, vbt_ref, kt_ref, vt_ref, lft_ref, u_ref, vn_ref, mq_ref = refs
    else:
        h_ref, q_ref, k_ref, v_ref, lf_ref, u_ref, vn_ref, mq_ref = refs
    hb = _rmsnorm(x_ref[...], g_ref[...]).astype(BF16)
    h_ref[...] = hb
    nt = (((1,), (1,)), ((), ()))

    def proj(c):
        return jnp.dot(hb, w_ref[:, c * 512:(c + 1) * 512], preferred_element_type=F32)

    q_ref[...] = (proj(0) * (FOX_HEAD_DIM ** -0.5 * LOG2E)).astype(BF16)
    if positions_minor:
        k_t = lax.dot_general(wf_ref[0:FOX_WIDTH, :], hb, nt, preferred_element_type=F32)
        kt_ref[...] = k_t
        kbt_ref[...] = k_t.astype(BF16)
        v_t = lax.dot_general(wf_ref[FOX_WIDTH:2 * FOX_WIDTH, :], hb, nt, preferred_element_type=F32)
        vt_ref[...] = v_t
        vbt_ref[...] = v_t.astype(BF16)
        f_t = lax.dot_general(wf_ref[2 * FOX_WIDTH:, :], hb, nt, preferred_element_type=F32) + bf_ref[...]
        lft_ref[...] = _log_sigmoid(f_t)
    else:
        k_ref[...] = proj(1)
        v_ref[...] = proj(2)
        lf_ref[...] = _log_sigmoid(jnp.dot(hb, wf_ref[...], preferred_element_type=F32) + bf_ref[...])
    u_ref[...] = _gelu(proj(3))
    vf = _gelu(proj(4))
    mu = jnp.mean(vf, axis=-1, keepdims=True)
    var = jnp.mean(jnp.square(vf - mu), axis=-1, keepdims=True)
    vn_ref[...] = (vf - mu) * lax.rsqrt(var + EPS) * gmg_ref[...]
    mq_ref[...] = proj(5)


def _proj_in(x, g, w_main, w_f, b_f, gm_g, tile0, ntiles, seq=None):
    d = x.shape[1]
    tm = TOKEN_TILE
    n = ntiles * tm
    row = lambda width: pl.BlockSpec((tm, width), lambda i: (i, 0))
    act = lambda width, dtype: jax.ShapeDtypeStruct((n, width), dtype)
    common_shapes = (act(GM_WIDTH, F32), act(GM_WIDTH, F32), act(MEM_WIDTH, F32))
    common_specs = (row(GM_WIDTH), row(GM_WIDTH), row(MEM_WIDTH))
    if seq is None:
        out_shapes = (act(d, BF16), act(FOX_WIDTH, BF16), act(FOX_WIDTH, F32), act(FOX_WIDTH, F32),
                      act(LANES, F32)) + common_shapes
        out_specs = (row(d), row(FOX_WIDTH), row(FOX_WIDTH), row(FOX_WIDTH), row(LANES)) + common_specs
    else:
        per_seq = seq // tm
        batch = n // seq
        t_spec = lambda ch: pl.BlockSpec((None, ch, tm), lambda i: (i // per_seq, 0, i % per_seq))
        t_shape = lambda ch, dtype: jax.ShapeDtypeStruct((batch, ch, seq), dtype)
        out_shapes = (act(d, BF16), act(FOX_WIDTH, BF16), t_shape(FOX_WIDTH, BF16), t_shape(FOX_WIDTH, BF16),
                      t_shape(FOX_WIDTH, F32), t_shape(FOX_WIDTH, F32), t_shape(LANES, F32)) + common_shapes
        out_specs = (row(d), row(FOX_WIDTH), t_spec(FOX_WIDTH), t_spec(FOX_WIDTH), t_spec(FOX_WIDTH),
                     t_spec(FOX_WIDTH), t_spec(LANES)) + common_specs
    return pl.pallas_call(
        functools.partial(_proj_in_kernel, positions_minor=seq is not None),
        grid=(ntiles,),
        in_specs=[pl.BlockSpec((tm, d), lambda i: (tile0 + i, 0)), _const_spec((1, d)), _const_spec(w_main.shape),
                  _const_spec(w_f.shape), _const_spec(b_f.shape), _const_spec((1, GM_WIDTH))],
        out_specs=out_specs,
        out_shape=out_shapes,
        compiler_params=_params("parallel"),
        name="proj_in_prompt" if seq is not None else "proj_in_sample",
    )(x, g, w_main, w_f, b_f, gm_g)


def _cumsum_kernel(lf_ref, o_ref, parts_ref, carry_ref):
    @pl.when(pl.program_id(1) == 0)
    def _():
        carry_ref[...] = jnp.zeros_like(carry_ref)

    tc = lf_ref.shape[-1]
    upper = (lax.broadcasted_iota(jnp.int32, (tc, tc), 0)
             <= lax.broadcasted_iota(jnp.int32, (tc, tc), 1)).astype(F32)
    y = jnp.dot(lf_ref[...], upper, precision=HIGHEST, preferred_element_type=F32) + carry_ref[...]
    carry_ref[...] = y[:, tc - 1:tc]
    y = y * LOG2E
    o_ref[...] = y
    hi = y.astype(BF16).astype(F32)
    mid = (y - hi).astype(BF16).astype(F32)
    parts_ref[0] = hi
    parts_ref[1] = mid
    parts_ref[2] = ((y - hi) - mid).astype(BF16).astype(F32)


def _cumsum_lanes(lf_t):
    b, h, s = lf_t.shape
    tc = min(CUMSUM_TILE, s)
    return pl.pallas_call(
        _cumsum_kernel,
        grid=(b, s // tc),
        in_specs=[pl.BlockSpec((None, h, tc), lambda i, j: (i, 0, j))],
        out_specs=(pl.BlockSpec((None, h, tc), lambda i, j: (i, 0, j)),
                   pl.BlockSpec((None, 3, h, tc), lambda i, j: (i, 0, 0, j))),
        out_shape=(jax.ShapeDtypeStruct((b, h, s), F32), jax.ShapeDtypeStruct((b, 3, h, s), F32)),
        scratch_shapes=[pltpu.VMEM((h, 1), F32)],
        compiler_params=_params("parallel", "arbitrary"),
        name="fox_cumsum",
    )(lf_t)


def _fox_attn_kernel(qi_tab, kj_tab, q_ref, k_ref, fa_ref, v_ref, fr_ref, o_ref,
                     qm_ref, fr_rep, m_ref, l_ref, alpha_ref, shift_ref, tmax_ref, acc_ref, s_ref, p_ref,
                     *, tq, tk):
    p = pl.program_id(2)
    qi = qi_tab[p]
    kj = kj_tab[p]
    lane = lax.broadcasted_iota(jnp.int32, (tq, LANES), 1)
    low = lane < FOX_HEAD_DIM
    n_col = tk // LANES
    n_chunk = tq // ATTN_ROWS

    @pl.when(kj == 0)
    def _():
        q = q_ref[...]
        for hh in range(2):
            qm_ref[hh, :, :LANES] = jnp.where(low if hh == 0 else jnp.logical_not(low), q, jnp.zeros_like(q))
            minus_one = (lane >= 3 * hh) & (lane < 3 * hh + 3)
            qm_ref[hh, :, LANES:] = jnp.where(minus_one, -1.0, 0.0).astype(BF16)
            fr_rep[hh] = jnp.broadcast_to(fr_ref[:, hh:hh + 1], (tq, LANES))
        m_ref[...] = jnp.full_like(m_ref, -jnp.inf)
        l_ref[...] = jnp.zeros_like(l_ref)
        acc_ref[...] = jnp.zeros_like(acc_ref)

    def step(masked):
        sub = lax.broadcasted_iota(jnp.int32, (ATTN_ROWS, LANES), 0)
        lane_r = lax.broadcasted_iota(jnp.int32, (ATTN_ROWS, LANES), 1)
        ka_t = jnp.concatenate([k_ref[...], fa_ref[...]], axis=0)
        for hh in range(2):
            s_ref[hh] = jnp.dot(qm_ref[hh], ka_t, preferred_element_type=F32)

        def scores(hh, r0):
            ts = []
            for j in range(n_col):
                t = s_ref[hh, pl.ds(r0, ATTN_ROWS), j * LANES:(j + 1) * LANES]
                if masked:
                    keep = (kj * tk + j * LANES + lane_r) <= (qi * tq + r0 + sub)
                    t = jnp.where(keep, t, -jnp.inf)
                ts.append(t)
            return ts

        for hh in range(2):
            def max_body(i, carry, hh=hh):
                r0 = pl.multiple_of(i * ATTN_ROWS, ATTN_ROWS)
                tmax_ref[hh, pl.ds(r0, ATTN_ROWS), :] = functools.reduce(jnp.maximum, scores(hh, r0))
                return carry

            lax.fori_loop(0, n_chunk, max_body, 0, unroll=ATTN_MAX_UNROLL)

        for hh in range(2):
            f_row = fr_rep[hh]
            m_prev = m_ref[hh]
            m_new = jnp.maximum(m_prev, jnp.max(tmax_ref[hh], axis=1, keepdims=True) + f_row)
            m_ref[hh] = m_new
            alpha_ref[hh] = jnp.exp2(m_prev - m_new)
            shift_ref[hh] = m_new - f_row

        v_t = v_ref[...]
        pvs = []
        for hh in range(2):
            def exp_body(i, carry, hh=hh):
                r0 = pl.multiple_of(i * ATTN_ROWS, ATTN_ROWS)
                rows = pl.ds(r0, ATTN_ROWS)
                shift = shift_ref[hh, rows, :]
                prs = [jnp.exp2(t - shift) for t in scores(hh, r0)]
                l_ref[hh, rows, :] = alpha_ref[hh, rows, :] * l_ref[hh, rows, :] + functools.reduce(jnp.add, prs)
                for j in range(n_col):
                    p_ref[hh, rows, j * LANES:(j + 1) * LANES] = prs[j].astype(BF16)
                return carry

            lax.fori_loop(0, n_chunk, exp_body, 0, unroll=ATTN_EXP_UNROLL)
            pvs.append(lax.dot_general(p_ref[hh], v_t, (((1,), (1,)), ((), ())), preferred_element_type=F32))
        acc = acc_ref[...]
        acc_ref[...] = jnp.where(low, alpha_ref[0] * acc + pvs[0], alpha_ref[1] * acc + pvs[1])

    on_diagonal = (kj + 1) * tk > qi * tq + 1
    pl.when(on_diagonal)(lambda: step(True))
    pl.when(jnp.logical_not(on_diagonal))(lambda: step(False))

    @pl.when((kj + 1) * tk >= (qi + 1) * tq)
    def _():
        l0 = jnp.sum(l_ref[0], axis=1, keepdims=True)
        l1 = jnp.sum(l_ref[1], axis=1, keepdims=True)
        o_ref[...] = (acc_ref[...] * jnp.where(low, 1.0 / l0, 1.0 / l1)).astype(BF16)


def _fox_prompt(q, kb_t, f_aug_t, vb_t, f_row, batch, seq):
    tq = min(ATTN_TQ, seq)
    tk = min(ATTN_TK, tq)
    nq = seq // tq
    pairs = [(i, j) for i in range(nq) for j in range(((i + 1) * tq) // tk)]
    qi_tab = jnp.asarray(np.array([a for a, _ in pairs], np.int32))
    kj_tab = jnp.asarray(np.array([c for _, c in pairs], np.int32))
    nqb = seq // tq
    grid_spec = pltpu.PrefetchScalarGridSpec(
        num_scalar_prefetch=2,
        grid=(batch, FOX_HEADS // 2, len(pairs)),
        in_specs=[
            pl.BlockSpec((tq, LANES), lambda b, h, p, qt, kt: (b * nqb + qt[p], h)),
            pl.BlockSpec((None, LANES, tk), lambda b, h, p, qt, kt: (b, h, kt[p])),
            pl.BlockSpec((None, None, LANES, tk), lambda b, h, p, qt, kt: (b, h, 0, kt[p])),
            pl.BlockSpec((None, LANES, tk), lambda b, h, p, qt, kt: (b, h, kt[p])),
            pl.BlockSpec((None, None, tq, 2), lambda b, h, p, qt, kt: (b, h, qt[p], 0)),
        ],
        out_specs=pl.BlockSpec((tq, LANES), lambda b, h, p, qt, kt: (b * nqb + qt[p], h)),
        scratch_shapes=[pltpu.VMEM((2, tq, 2 * LANES), BF16),
                        pltpu.VMEM((2, tq, LANES), F32),
                        pltpu.VMEM((2, tq, LANES), F32),
                        pltpu.VMEM((2, tq, LANES), F32),
                        pltpu.VMEM((2, tq, LANES), F32),
                        pltpu.VMEM((2, tq, LANES), F32),
                        pltpu.VMEM((2, tq, LANES), F32),
                        pltpu.VMEM((tq, LANES), F32),
                        pltpu.VMEM((2, tq, tk), F32),
                        pltpu.VMEM((2, tq, tk), BF16)],
    )
    return pl.pallas_call(
        functools.partial(_fox_attn_kernel, tq=tq, tk=tk),
        grid_spec=grid_spec,
        out_shape=jax.ShapeDtypeStruct((batch * seq, FOX_WIDTH), BF16),
        compiler_params=_params("parallel", "parallel", "arbitrary"),
        name="fox_prompt_attn",
    )(qi_tab, kj_tab, q, kb_t, f_aug_t, vb_t, f_row)


def _fox_sample_kernel(pt_ref, q_ref, kn_ref, vn_ref, lfn_ref, *refs, n_pages, t_new):
    k_refs = refs[:n_pages]
    v_refs = refs[n_pages:2 * n_pages]
    lf_refs = refs[2 * n_pages:3 * n_pages]
    o_ref = refs[3 * n_pages]
    t_ref = refs[3 * n_pages + 1]
    rows = FOX_HEADS * t_new
    nt = (((1,), (1,)), ((), ()))

    q = q_ref[...]
    qt = jnp.concatenate([q] * FOX_HEADS, axis=0)
    r_w = lax.broadcasted_iota(jnp.int32, (rows, FOX_WIDTH), 0)
    c_w = lax.broadcasted_iota(jnp.int32, (rows, FOX_WIDTH), 1)
    head_match = _div(r_w, t_new) == _div(c_w, FOX_HEAD_DIM)
    q_bd = jnp.where(head_match, qt, 0.0).astype(BF16)
    upper = (lax.broadcasted_iota(jnp.int32, (PAGE_SIZE, PAGE_SIZE), 0)
             <= lax.broadcasted_iota(jnp.int32, (PAGE_SIZE, PAGE_SIZE), 1)).astype(F32)
    r_p = lax.broadcasted_iota(jnp.int32, (rows, PAGE_SIZE), 0)
    c_p = lax.broadcasted_iota(jnp.int32, (rows, PAGE_SIZE), 1)
    pad = jnp.zeros((PAGE_SIZE - t_new, FOX_WIDTH), F32)

    carry = jnp.zeros((FOX_HEADS, 1), F32)
    m = jnp.full((rows, 1), -jnp.inf, F32)
    f_row = None
    for pg in range(n_pages + 1):
        if pg < n_pages:
            k_t = k_refs[pg][...].reshape(FOX_WIDTH, PAGE_SIZE).astype(BF16)
            s = jnp.dot(q_bd, k_t, preferred_element_type=F32)
            lf_t = lf_refs[pg][...]
        else:
            k_new = jnp.concatenate([kn_ref[...], pad], axis=0).astype(BF16)
            s = lax.dot_general(q_bd, k_new, nt, preferred_element_type=F32)
            lf_t = lfn_ref[...]
        f_t = jnp.dot(lf_t, upper, precision=HIGHEST, preferred_element_type=F32) + carry
        carry = f_t[:, PAGE_SIZE - 1:PAGE_SIZE]
        f_t = f_t * LOG2E
        f_col = jnp.concatenate(
            [jnp.broadcast_to(f_t[h:h + 1, :], (t_new, PAGE_SIZE)) for h in range(FOX_HEADS)], axis=0)
        t = s - f_col
        if pg == n_pages:
            t = jnp.where(c_p <= _mod(r_p, t_new), t, -jnp.inf)
            f_row = jnp.sum(jnp.where(c_p == _mod(r_p, t_new), f_col, 0.0), axis=1, keepdims=True)
        t_ref[pg] = t
        m = jnp.maximum(m, jnp.max(t, axis=1, keepdims=True))
    m_logit = m + f_row
    shift = m_logit - f_row
    l = jnp.zeros((rows, 1), F32)
    acc = jnp.zeros((rows, FOX_WIDTH), F32)
    for pg in range(n_pages + 1):
        pr = jnp.exp2(t_ref[pg] - shift)
        l = l + jnp.sum(pr, axis=1, keepdims=True)
        if pg < n_pages:
            v_t = v_refs[pg][...].reshape(FOX_WIDTH, PAGE_SIZE).astype(BF16)
            acc = acc + lax.dot_general(pr.astype(BF16), v_t, nt, preferred_element_type=F32)
        else:
            v_new = jnp.concatenate([vn_ref[...], pad], axis=0).astype(BF16)
            acc = acc + jnp.dot(pr.astype(BF16), v_new, preferred_element_type=F32)
    o = acc / l
    c_o = lax.broadcasted_iota(jnp.int32, (t_new, FOX_WIDTH), 1)
    out = jnp.zeros((t_new, FOX_WIDTH), F32)
    for h in range(FOX_HEADS):
        out = jnp.where(_div(c_o, FOX_HEAD_DIM) == h, o[h * t_new:(h + 1) * t_new, :], out)
    o_ref[...] = out


def _fox_sample(page_table, q32, k32, v32, lf_new_t, cache_kt, cache_vt, cache_lf_t, layer, row0, t_new):
    nb, n_pages = page_table.shape
    blk0 = row0 // t_new
    tok = pl.BlockSpec((t_new, FOX_WIDTH), lambda b, pt: (blk0 + b, 0))

    def kv_spec(pg):
        return pl.BlockSpec((None, None, FOX_HEADS, FOX_HEAD_DIM, PAGE_SIZE),
                            lambda b, pt: (layer, pt[b, pg], 0, 0, 0))

    def lf_spec(pg):
        return pl.BlockSpec((None, None, FOX_HEADS, PAGE_SIZE), lambda b, pt: (layer, pt[b, pg], 0, 0))

    in_specs = [tok, tok, tok, pl.BlockSpec((None, FOX_HEADS, LANES), lambda b, pt: (b, 0, 0))]
    in_specs += [kv_spec(pg) for pg in range(n_pages)] * 2
    in_specs += [lf_spec(pg) for pg in range(n_pages)]
    grid_spec = pltpu.PrefetchScalarGridSpec(
        num_scalar_prefetch=1,
        grid=(nb,),
        in_specs=in_specs,
        out_specs=pl.BlockSpec((t_new, FOX_WIDTH), lambda b, pt: (b, 0)),
        scratch_shapes=[pltpu.VMEM((n_pages + 1, FOX_HEADS * t_new, PAGE_SIZE), F32)],
    )
    return pl.pallas_call(
        functools.partial(_fox_sample_kernel, n_pages=n_pages, t_new=t_new),
        grid_spec=grid_spec,
        out_shape=jax.ShapeDtypeStruct((nb * t_new, FOX_WIDTH), F32),
        compiler_params=_params("parallel"),
        name="fox_sample_attn",
    )(page_table, q32, k32, v32, lf_new_t, *([cache_kt] * n_pages), *([cache_vt] * n_pages),
      *([cache_lf_t] * n_pages))


def _norm_matmul_kernel(x_ref, g_ref, w_ref, o_ref):
    hb = _rmsnorm(x_ref[...], g_ref[...]).astype(BF16)
    o_ref[...] = jnp.dot(hb, w_ref[...], preferred_element_type=F32)


def _norm_matmul(x, g, w):
    n, d = x.shape
    tm = min(TOKEN_TILE, n)
    return pl.pallas_call(
        _norm_matmul_kernel,
        grid=(n // tm,),
        in_specs=[pl.BlockSpec((tm, d), lambda i: (i, 0)), _const_spec((1, d)), _const_spec(w.shape)],
        out_specs=pl.BlockSpec((tm, w.shape[1]), lambda i: (i, 0)),
        out_shape=jax.ShapeDtypeStruct((n, w.shape[1]), F32),
        compiler_params=_params("parallel"),
        name="mem_kv",
    )(x, g, w)


def _mem_attend_kernel(q_ref, mk_ref, mv_ref, o_ref):
    scale = MEM_HEAD_DIM ** -0.5
    n_mem = mk_ref.shape[0]
    t = q_ref.shape[0] // n_mem
    for g in range(n_mem):
        rs = slice(g * t, (g + 1) * t)
        for h in range(MEM_HEADS):
            sl = slice(h * MEM_HEAD_DIM, (h + 1) * MEM_HEAD_DIM)
            s = lax.dot_general(q_ref[rs, sl].astype(BF16), mk_ref[g, :, sl].astype(BF16),
                                (((1,), (1,)), ((), ())), preferred_element_type=F32) * scale
            e = jnp.exp(s - jnp.max(s, axis=1, keepdims=True))
            pr = e / jnp.sum(e, axis=1, keepdims=True)
            o_ref[rs, sl] = jnp.dot(pr.astype(BF16), mv_ref[g, :, sl].astype(BF16), preferred_element_type=F32)


def _mem_attend_few_rows_kernel(q_ref, mk_ref, mv_ref, o_ref):
    scale = MEM_HEAD_DIM ** -0.5
    n_mem, m_len = mk_ref.shape[0], mk_ref.shape[1]
    t = q_ref.shape[0] // n_mem
    rows, width = MEM_HEADS * t, m_len * MEM_HEADS
    r = lax.broadcasted_iota(jnp.int32, (rows, width), 0)
    c = lax.broadcasted_iota(jnp.int32, (rows, width), 1)
    valid = _mod(c, MEM_HEADS) == _div(r, t)
    for g in range(n_mem):
        rs = slice(g * t, (g + 1) * t)
        q_ht = jnp.concatenate([q_ref[rs, h * MEM_HEAD_DIM:(h + 1) * MEM_HEAD_DIM] for h in range(MEM_HEADS)],
                               axis=0).astype(BF16)
        mk2 = mk_ref[g].reshape(width, MEM_HEAD_DIM).astype(BF16)
        s = lax.dot_general(q_ht, mk2, (((1,), (1,)), ((), ())), preferred_element_type=F32) * scale
        s = jnp.where(valid, s, -jnp.inf)
        e = jnp.exp(s - jnp.max(s, axis=1, keepdims=True))
        pr = e / jnp.sum(e, axis=1, keepdims=True)
        o = jnp.dot(pr.astype(BF16), mv_ref[g].reshape(width, MEM_HEAD_DIM).astype(BF16),
                    preferred_element_type=F32)
        for h in range(MEM_HEADS):
            o_ref[rs, h * MEM_HEAD_DIM:(h + 1) * MEM_HEAD_DIM] = o[h * t:(h + 1) * t, :]


def _mem_attend(mq, mk, mv, row0, n_mem, t_per_mem, name, layer=None):
    tm = min(TOKEN_TILE * 2, t_per_mem)
    tiles = t_per_mem // tm
    group = max(1, min(MEM_GROUP_ROWS // tm, n_mem)) if tiles == 1 else 1
    assert n_mem % group == 0
    rows = tm * group
    blk0 = row0 // rows
    if layer is None:
        assert group == 1
        mem_spec = pl.BlockSpec((1, mk.shape[1], MEM_WIDTH), lambda b, i: (b, 0, 0))
    else:
        assert group > 1
        mem_spec = pl.BlockSpec((None, group, mk.shape[2], MEM_HEADS, MEM_HEAD_DIM),
                                lambda b, i: (layer, b, 0, 0, 0))
    return pl.pallas_call(
        _mem_attend_few_rows_kernel if group > 1 else _mem_attend_kernel,
        grid=(n_mem // group, tiles),
        in_specs=[pl.BlockSpec((rows, MEM_WIDTH), lambda b, i: (blk0 + b * tiles + i, 0)), mem_spec, mem_spec],
        out_specs=pl.BlockSpec((rows, MEM_WIDTH), lambda b, i: (b * tiles + i, 0)),
        out_shape=jax.ShapeDtypeStruct((n_mem * t_per_mem, MEM_WIDTH), F32),
        compiler_params=_params("parallel", "parallel"),
        name=name,
    )(mq, mk, mv)


def _merge_kernel(x_ref, h_ref, fox_ref, u_ref, vn_ref, mem_ref, ws_ref, bs_ref,
                  wg_ref, bg_ref, wbr_ref, wo_ref, o_ref, gm_ref, *, chunk):
    tm, d = x_ref.shape
    r_c = lax.broadcasted_iota(jnp.int32, (CHUNK, CHUNK), 0)
    c_c = lax.broadcasted_iota(jnp.int32, (CHUNK, CHUNK), 1)
    keep = (c_c <= r_c) & (_div(r_c, chunk) == _div(c_c, chunk))
    for g in range(GM_GROUPS):
        w = jnp.where(keep, ws_ref[g], 0.0).astype(BF16)
        sl = slice(g * GM_GROUP_DIM, (g + 1) * GM_GROUP_DIM)
        for c in range(tm // CHUNK):
            rs = slice(c * CHUNK, (c + 1) * CHUNK)
            mix = jnp.dot(w, vn_ref[rs, sl].astype(BF16), preferred_element_type=F32) + bs_ref[:, g:g + 1]
            gm_ref[rs, sl] = (u_ref[rs, sl] * mix).astype(BF16)

    hb = h_ref[...]
    branches = (fox_ref[...].astype(BF16), gm_ref[...], mem_ref[...].astype(BF16))
    merged = None
    for r in range(N_BRANCH):
        gate = jax.nn.sigmoid(jnp.dot(hb, wg_ref[:, r * d:(r + 1) * d], preferred_element_type=F32)
                              + bg_ref[:, r * d:(r + 1) * d])
        term = gate * jnp.dot(branches[r], wbr_ref[r], preferred_element_type=F32)
        merged = term if merged is None else merged + term
    o_ref[...] = x_ref[...] + jnp.dot(merged.astype(BF16), wo_ref[...], preferred_element_type=F32)


def _merge(x, x_tile0, h, fox_o, u, vn, mem_o, ws, bs_t, w_gate, b_gate, w_br, w_o, ntiles, chunk, name):
    d = x.shape[1]
    tm = TOKEN_TILE
    own = lambda width: pl.BlockSpec((tm, width), lambda i: (i, 0))
    in_specs = [pl.BlockSpec((tm, d), lambda i: (x_tile0 + i, 0)), own(d), own(FOX_WIDTH), own(GM_WIDTH),
                own(GM_WIDTH), own(MEM_WIDTH), _const_spec(ws.shape), _const_spec(bs_t.shape),
                _const_spec(w_gate.shape), _const_spec((1, N_BRANCH * d)), _const_spec(w_br.shape),
                _const_spec(w_o.shape)]
    return pl.pallas_call(
        functools.partial(_merge_kernel, chunk=chunk),
        grid=(ntiles,),
        in_specs=in_specs,
        out_specs=pl.BlockSpec((tm, d), lambda i: (i, 0)),
        out_shape=jax.ShapeDtypeStruct((ntiles * tm, d), F32),
        scratch_shapes=[pltpu.VMEM((tm, GM_WIDTH), BF16)],
        compiler_params=_params("parallel"),
        name=name,
    )(x, h, fox_o, u, vn, mem_o, ws, bs_t, w_gate, b_gate, w_br, w_o)


def _moe_route_kernel(x_ref, g_ref, wr_ref, br_ref, hn_ref, meta_ref, cnt_ref, carry_ref):
    @pl.when(pl.program_id(0) == 0)
    def _():
        carry_ref[...] = jnp.zeros_like(carry_ref)

    tm = x_ref.shape[0]
    hn = _rmsnorm(x_ref[...], g_ref[...])
    hn_ref[...] = hn
    logits = jnp.dot(hn, wr_ref[...], precision=HIGHEST, preferred_element_type=F32) + br_ref[...]
    col = lax.broadcasted_iota(jnp.int32, (tm, LANES), 1)
    colf = col.astype(F32)
    big = jnp.float32(1e9)

    def masked_softmax(mask):
        z = jnp.where(mask, logits, -jnp.inf)
        e = jnp.exp(z - jnp.max(z, axis=1, keepdims=True))
        return e / jnp.sum(e, axis=1, keepdims=True)

    def first_argmax(vals, vmax, mask):
        return jnp.min(jnp.where(mask & (vals == vmax), colf, big), axis=1, keepdims=True)

    is_group = col < N_GROUPS
    g_prob = masked_softmax(is_group)
    p_g = jnp.max(g_prob, axis=1, keepdims=True)
    g_idx = first_argmax(g_prob, p_g, is_group)
    e_col = col - N_GROUPS
    in_group = (e_col >= 0) & (e_col < N_EXPERTS) & (_div(e_col, EXPERTS_PER_GROUP).astype(F32) == g_idx)
    e_prob = masked_softmax(in_group)
    p1 = jnp.max(e_prob, axis=1, keepdims=True)
    i1 = first_argmax(e_prob, p1, in_group)
    rest = in_group & (colf != i1)
    p2 = jnp.max(jnp.where(rest, e_prob, -1.0), axis=1, keepdims=True)
    i2 = first_argmax(e_prob, p2, rest)
    denom = p1 + p2
    w1 = p1 / denom * p_g
    w2 = p2 / denom * p_g
    e1 = i1 - N_GROUPS
    e2 = i2 - N_GROUPS

    onehot = ((colf == e1) | (colf == e2)).astype(F32)
    lower = (lax.broadcasted_iota(jnp.int32, (tm, tm), 1)
             < lax.broadcasted_iota(jnp.int32, (tm, tm), 0)).astype(BF16)
    before = jnp.dot(lower, onehot.astype(BF16), preferred_element_type=F32) + carry_ref[...]
    r1 = jnp.sum(jnp.where(colf == e1, before, 0.0), axis=1, keepdims=True)
    r2 = jnp.sum(jnp.where(colf == e2, before, 0.0), axis=1, keepdims=True)
    carry_ref[...] = carry_ref[...] + jnp.sum(onehot, axis=0, keepdims=True)
    cnt_ref[...] = jnp.broadcast_to(carry_ref[...], cnt_ref.shape)

    meta = jnp.zeros((tm, LANES), F32)
    for lane_id, val in enumerate((e1, e2, r1, r2, w1, w2)):
        meta = jnp.where(col == lane_id, val, meta)
    meta_ref[...] = meta


def _moe_route(x, g, w_route, b_route):
    n, d = x.shape
    tm = ROUTE_TILE
    return pl.pallas_call(
        _moe_route_kernel,
        grid=(n // tm,),
        in_specs=[pl.BlockSpec((tm, d), lambda i: (i, 0)), _const_spec((1, d)),
                  _const_spec(w_route.shape), _const_spec((1, LANES))],
        out_specs=(pl.BlockSpec((tm, d), lambda i: (i, 0)), pl.BlockSpec((tm, LANES), lambda i: (i, 0)),
                   _const_spec((8, LANES))),
        out_shape=(jax.ShapeDtypeStruct((n, d), F32), jax.ShapeDtypeStruct((n, LANES), F32),
                   jax.ShapeDtypeStruct((8, LANES), F32)),
        scratch_shapes=[pltpu.VMEM((1, LANES), F32)],
        compiler_params=_params("arbitrary"),
        name="moe_route",
    )(x, g, w_route, b_route)


def _moe_expert_kernel(te_ref, tv_ref, src_ref, src_next_ref, hn_hbm, wgu_ref, wdn_ref, ys_ref,
                       x_buf, wgu_bf, wdn_bf, sem, *, tg):
    t = pl.program_id(0)
    n_t = pl.num_programs(0)
    slot = t % 2

    def gather(idx_ref, buf_slot):
        def issue(i, c):
            base = pl.multiple_of(i * DMA_UNROLL, DMA_UNROLL)
            for k in range(DMA_UNROLL):
                pltpu.make_async_copy(hn_hbm.at[pl.ds(idx_ref[0, base + k], 1)],
                                      x_buf.at[buf_slot, pl.ds(base + k, 1)], sem.at[buf_slot]).start()
            return c

        lax.fori_loop(0, tg // DMA_UNROLL, issue, 0)

    @pl.when((t == 0) & (tv_ref[0] == 1))
    def _():
        gather(src_ref, 0)

    @pl.when((t + 1 < n_t) & (tv_ref[jnp.minimum(t + 1, n_t - 1)] == 1))
    def _():
        gather(src_next_ref, 1 - slot)

    @pl.when(tv_ref[t] == 1)
    def _():
        @pl.when((t == 0) | (te_ref[t] != te_ref[jnp.maximum(t - 1, 0)]))
        def _():
            wgu_bf[...] = wgu_ref[...].astype(BF16)
            wdn_bf[...] = wdn_ref[...].astype(BF16)

        pltpu.make_async_copy(hn_hbm.at[pl.ds(0, tg)], x_buf.at[slot], sem.at[slot]).wait()
        gu = jnp.dot(x_buf[slot].astype(BF16), wgu_bf[...], preferred_element_type=F32)
        gate = gu[:, :EXPERT_FF]
        act = gate * jax.nn.sigmoid(gate) * gu[:, EXPERT_FF:]
        ys_ref[...] = jnp.dot(act.astype(BF16), wdn_bf[...], preferred_element_type=F32)

    @pl.when(tv_ref[t] != 1)
    def _():
        ys_ref[...] = jnp.zeros_like(ys_ref)


def _moe_experts(tile_expert, tile_valid, src_idx, hn, w_gu, w_dn, layer):
    n_tiles = tile_expert.shape[0]
    tg = MOE_TILE
    d = hn.shape[1]
    grid_spec = pltpu.PrefetchScalarGridSpec(
        num_scalar_prefetch=2,
        grid=(n_tiles,),
        in_specs=[
            pl.BlockSpec((None, 1, tg), lambda t, te, tv: (t, 0, 0), memory_space=pltpu.SMEM),
            pl.BlockSpec((None, 1, tg), lambda t, te, tv: (jnp.minimum(t + 1, n_tiles - 1), 0, 0),
                         memory_space=pltpu.SMEM),
            pl.BlockSpec(memory_space=pl.ANY),
            pl.BlockSpec((None, None, d, 2 * EXPERT_FF), lambda t, te, tv: (layer, te[t], 0, 0)),
            pl.BlockSpec((None, None, EXPERT_FF, d), lambda t, te, tv: (layer, te[t], 0, 0)),
        ],
        out_specs=pl.BlockSpec((tg, d), lambda t, te, tv: (t, 0)),
        scratch_shapes=[pltpu.VMEM((2, tg, d), F32), pltpu.VMEM((d, 2 * EXPERT_FF), BF16),
                        pltpu.VMEM((EXPERT_FF, d), BF16), pltpu.SemaphoreType.DMA((2,))],
    )
    return pl.pallas_call(
        functools.partial(_moe_expert_kernel, tg=tg),
        grid_spec=grid_spec,
        out_shape=jax.ShapeDtypeStruct((n_tiles * tg, d), F32),
        compiler_params=_params("arbitrary"),
        name="moe_experts",
    )(tile_expert, tile_valid, src_idx, src_idx, hn, w_gu, w_dn)


def _moe_combine_kernel(dest_ref, x_ref, meta_ref, fg_ref, ys_hbm, o_ref, buf, sem, *, tm, final):
    def issue(i, c):
        base = pl.multiple_of(i * DMA_UNROLL, DMA_UNROLL)
        for k in range(DMA_UNROLL):
            for slot in range(2):
                pltpu.make_async_copy(ys_hbm.at[pl.ds(dest_ref[0, 2 * (base + k) + slot], 1)],
                                      buf.at[slot, pl.ds(base + k, 1)], sem.at[slot]).start()
        return c

    lax.fori_loop(0, tm // DMA_UNROLL, issue, 0)
    for slot in range(2):
        pltpu.make_async_copy(ys_hbm.at[pl.ds(0, tm)], buf.at[slot], sem.at[slot]).wait()
    meta = meta_ref[...]
    y = x_ref[...] + (meta[:, 4:5] * buf[0] + meta[:, 5:6] * buf[1])
    if final:
        y = _rmsnorm(y, fg_ref[...])
    o_ref[...] = y


def _moe_combine(dest, x, meta, final_g, ys, final, tile0, ntiles):
    d = x.shape[1]
    tm = TOKEN_TILE
    return pl.pallas_call(
        functools.partial(_moe_combine_kernel, tm=tm, final=final),
        grid=(ntiles,),
        in_specs=[pl.BlockSpec((None, 1, 2 * tm), lambda i: (tile0 + i, 0, 0), memory_space=pltpu.SMEM),
                  pl.BlockSpec((tm, d), lambda i: (tile0 + i, 0)),
                  pl.BlockSpec((tm, LANES), lambda i: (tile0 + i, 0)),
                  _const_spec((1, d)), pl.BlockSpec(memory_space=pl.ANY)],
        out_specs=pl.BlockSpec((tm, d), lambda i: (i, 0)),
        out_shape=jax.ShapeDtypeStruct((ntiles * tm, d), F32),
        scratch_shapes=[pltpu.VMEM((2, tm, d), F32), pltpu.SemaphoreType.DMA((2,))],
        compiler_params=_params("arbitrary"),
        name="moe_combine",
    )(dest, x, meta, final_g, ys)


def _moe_layer(x, g, w_route, b_route, w_gu, w_dn, layer, final_g, final, splits):
    n, d = x.shape
    tg = MOE_TILE
    hn, meta, counts = _moe_route(x, g, w_route, b_route)
    counts = counts[0, :N_EXPERTS].astype(jnp.int32)
    padded = ((counts + tg - 1) // tg) * tg
    ends = jnp.cumsum(padded)
    starts = ends - padded
    eid = meta[:, 0:2].astype(jnp.int32)
    rank = meta[:, 2:4].astype(jnp.int32)
    dest = starts[eid] + rank
    n_tiles = (2 * n) // tg + N_EXPERTS
    tile_start = jnp.arange(n_tiles, dtype=jnp.int32) * tg
    tile_expert = jnp.minimum(jnp.sum((ends[None, :] <= tile_start[:, None]).astype(jnp.int32), axis=1),
                              N_EXPERTS - 1)
    tile_valid = (tile_start < ends[-1]).astype(jnp.int32)
    token = jnp.broadcast_to(jnp.arange(n, dtype=jnp.int32)[:, None], (n, 2))
    src_idx = jnp.zeros((n_tiles * tg,), jnp.int32).at[dest.reshape(-1)].set(token.reshape(-1))
    ys = _moe_experts(tile_expert, tile_valid, src_idx.reshape(n_tiles, 1, tg), hn, w_gu, w_dn, layer)
    dest_tiles = dest.reshape(n // TOKEN_TILE, 1, 2 * TOKEN_TILE)
    return tuple(_moe_combine(dest_tiles, x, meta, final_g, ys, final, t0, nt) for t0, nt in splits)


def kernel(x_prompt, x_sample, cache_k, cache_v, cache_logf, cache_mem_k, cache_mem_v, page_table, mem_prompt, norm1_g, w_in, b_f, gm_norm_g, gm_ws, gm_bs, mem_norm_g, w_mem_kv, w_gate, b_gate, w_br, w_o, norm2_g, w_rg, b_rg, w_re, b_re, w_moe_gu, w_moe_dn, final_g):
    batch, seq, d = x_prompt.shape
    dec_batch, dec_seq, _ = x_sample.shape
    depth = w_in.shape[0]
    n_pool = cache_k.shape[1]
    mem_len = mem_prompt.shape[1]
    n_p = batch * seq
    n_s = dec_batch * dec_seq
    assert n_p % (2 * TOKEN_TILE) == 0 and n_s % (2 * TOKEN_TILE) == 0 and seq % CHUNK == 0
    assert CHUNK % dec_seq == 0 and (n_p + n_s) % ROUTE_TILE == 0
    n_prompt_tiles = n_p // TOKEN_TILE

    n_sample_tiles = n_s // TOKEN_TILE
    n_all = n_p + n_s
    x_of = {"prompt": (x_prompt.reshape(n_p, d), 0), "sample": (x_sample.reshape(n_s, d), 0)}
    mem_flat = mem_prompt.reshape(batch * mem_len, d)
    ckt = jnp.transpose(cache_k, (0, 1, 3, 4, 2))
    cvt = jnp.transpose(cache_v, (0, 1, 3, 4, 2))
    clf_t = jnp.swapaxes(cache_logf, 2, 3)

    q_end, f_end = 3 * FOX_WIDTH, 3 * FOX_WIDTH + FOX_HEADS
    reps = CHUNK // dec_seq
    outs = {name: [] for name in ("kp", "vp", "fp", "mkp", "mvp", "ks", "vs", "fs", "gs")}
    for l in range(depth):
        w_main = jnp.concatenate([w_in[l][:, :q_end], w_in[l][:, f_end:]], axis=1).astype(BF16)
        w_f = jnp.pad(w_in[l][:, q_end:f_end], ((0, 0), (0, LANES - FOX_HEADS))).astype(BF16)
        bf_pad = jnp.pad(b_f[l], (0, LANES - FOX_HEADS)).reshape(1, LANES)
        ws_sample = jnp.tile(gm_ws[l][:, :dec_seq, :dec_seq], (1, reps, reps))
        bs_sample_t = jnp.tile(gm_bs[l][:, :dec_seq], (1, reps)).T
        w_route = jnp.pad(jnp.concatenate([w_rg[l], w_re[l]], axis=1),
                          ((0, 0), (0, LANES - N_GROUPS - N_EXPERTS)))
        b_route = jnp.pad(jnp.concatenate([b_rg[l], b_re[l]]), (0, LANES - N_GROUPS - N_EXPERTS)).reshape(1, LANES)

        w_kvf_t = jnp.concatenate([w_main[:, FOX_WIDTH:3 * FOX_WIDTH], w_f], axis=1).T
        norm_g, gm_g = norm1_g[l].reshape(1, d), gm_norm_g[l].reshape(1, GM_WIDTH)

        h_p, q_p, kb_t, vb_t, k_t, v_t, lf_t, u_p, vn_p, mq_p = _proj_in(
            x_of["prompt"][0], norm_g, w_main, w_kvf_t, bf_pad.reshape(LANES, 1), gm_g, x_of["prompt"][1],
            n_prompt_tiles, seq=seq)
        h_s, q_s, k_s, v_s, lf_s, u_s, vn_s, mq_s = _proj_in(
            x_of["sample"][0], norm_g, w_main, w_f, bf_pad, gm_g, x_of["sample"][1], n_sample_tiles)

        lf_p_t = lf_t[:, :FOX_HEADS, :]
        f_t, f_parts = _cumsum_lanes(lf_p_t)
        f_row = jnp.swapaxes(f_t.reshape(batch, FOX_HEADS // 2, 2, seq), 2, 3)
        f_aug_t = jnp.swapaxes(f_parts, 1, 2).reshape(batch, FOX_HEADS // 2, 6, seq)
        f_aug_t = jnp.pad(f_aug_t, ((0, 0), (0, 0), (0, LANES - 6), (0, 0))).astype(BF16)
        fox_p = _fox_prompt(q_p, kb_t, f_aug_t, vb_t, f_row, batch, seq)

        lf_s = lf_s[:, :FOX_HEADS].reshape(dec_batch, dec_seq, FOX_HEADS)
        lf_new_t = jnp.pad(jnp.swapaxes(lf_s, 1, 2), ((0, 0), (0, 0), (0, LANES - dec_seq)))
        fox_s = _fox_sample(page_table, q_s.astype(F32), k_s, v_s, lf_new_t, ckt, cvt, clf_t, l, 0, dec_seq)

        kv = _norm_matmul(mem_flat, mem_norm_g[l].reshape(1, d), w_mem_kv[l].astype(BF16))
        mk_p = kv[:, :MEM_WIDTH].reshape(batch, mem_len, MEM_WIDTH)
        mv_p = kv[:, MEM_WIDTH:].reshape(batch, mem_len, MEM_WIDTH)
        mem_p = _mem_attend(mq_p, mk_p, mv_p, 0, batch, seq, "mem_attend_prompt")
        mem_s = _mem_attend(mq_s, cache_mem_k, cache_mem_v, 0, dec_batch, dec_seq, "mem_attend_sample", layer=l)

        merge_w = (w_gate[l].astype(BF16), b_gate[l].reshape(1, N_BRANCH * d), w_br[l].astype(BF16),
                   w_o[l].astype(BF16))
        x_p = _merge(*x_of["prompt"], h_p, fox_p, u_p, vn_p, mem_p, gm_ws[l], gm_bs[l].T, *merge_w,
                     n_prompt_tiles, CHUNK, "merge_prompt")
        x_s = _merge(*x_of["sample"], h_s, fox_s, u_s, vn_s, mem_s, ws_sample, bs_sample_t, *merge_w,
                     n_sample_tiles, dec_seq, "merge_sample")
        x_mid = jnp.concatenate([x_p, x_s], axis=0)
        last = l == depth - 1
        splits = ((0, n_prompt_tiles), (n_prompt_tiles, n_sample_tiles)) if last else ((0, n_all // TOKEN_TILE),)
        x_out = _moe_layer(x_mid, norm2_g[l].reshape(1, d), w_route, b_route, w_moe_gu, w_moe_dn, l,
                           final_g.reshape(1, d), last, splits)
        if last:
            x_of = {"prompt": (x_out[0], 0), "sample": (x_out[1], 0)}
        else:
            x_of = {"prompt": (x_out[0], 0), "sample": (x_out[0], n_prompt_tiles)}

        to_bshd = lambda a: jnp.transpose(a.reshape(batch, FOX_HEADS, FOX_HEAD_DIM, seq), (0, 3, 1, 2))
        outs["kp"].append(to_bshd(k_t))
        outs["vp"].append(to_bshd(v_t))
        outs["fp"].append(jnp.swapaxes(lf_p_t, 1, 2))
        outs["mkp"].append(mk_p.reshape(batch, mem_len, MEM_HEADS, MEM_HEAD_DIM))
        outs["mvp"].append(mv_p.reshape(batch, mem_len, MEM_HEADS, MEM_HEAD_DIM))
        outs["ks"].append(k_s.reshape(dec_batch, dec_seq, FOX_HEADS, FOX_HEAD_DIM))
        outs["vs"].append(v_s.reshape(dec_batch, dec_seq, FOX_HEADS, FOX_HEAD_DIM))
        outs["fs"].append(lf_s)
        outs["gs"].append(vn_s.reshape(dec_batch, dec_seq, GM_WIDTH))

    y_prompt = x_of["prompt"][0].reshape(batch, seq, d)
    y_sample = x_of["sample"][0].reshape(dec_batch, dec_seq, d)
    return (y_prompt, y_sample) + tuple(
        jnp.stack(outs[name]) for name in ("kp", "vp", "fp", "mkp", "mvp", "ks", "vs", "fs", "gs"))
```

```python
import functools

import numpy as np
import jax
import jax.numpy as jnp
from jax import lax
from jax.experimental import pallas as pl
from jax.experimental.pallas import tpu as pltpu

F32 = jnp.float32
BF16 = jnp.bfloat16
HIGHEST = lax.Precision.HIGHEST

EPS = 1e-6
LOG2E = 1.4426950408889634
FOX_HEADS = 8
FOX_HEAD_DIM = 64
FOX_WIDTH = FOX_HEADS * FOX_HEAD_DIM
GM_GROUPS = 4
GM_GROUP_DIM = 128
GM_WIDTH = GM_GROUPS * GM_GROUP_DIM
CHUNK = 128
MEM_HEADS = 4
MEM_HEAD_DIM = 128
MEM_WIDTH = MEM_HEADS * MEM_HEAD_DIM
N_BRANCH = 3
N_GROUPS = 4
EXPERTS_PER_GROUP = 8
N_EXPERTS = N_GROUPS * EXPERTS_PER_GROUP
EXPERT_FF = 512
PAGE_SIZE = 128

LANES = 128
VMEM_LIMIT_BYTES = 56 * 1024 * 1024
TOKEN_TILE = 256
ATTN_TQ = 1024
ATTN_TK = 1024
ATTN_ROWS = 32
ATTN_MAX_UNROLL = 64
ATTN_EXP_UNROLL = 64
CUMSUM_TILE = 512
MOE_TILE = 256
ROUTE_TILE = 512
MEM_GROUP_ROWS = 64
DMA_UNROLL = 8


def _params(*sem):
    return pltpu.CompilerParams(dimension_semantics=sem, vmem_limit_bytes=VMEM_LIMIT_BYTES)


def _rmsnorm(x, g):
    return x * lax.rsqrt(jnp.mean(x * x, axis=-1, keepdims=True) + EPS) * g


def _gelu(x):
    return 0.5 * x * (1.0 + jnp.tanh(np.sqrt(2.0 / np.pi).astype(np.float32) * (x + 0.044715 * (x * x * x))))


def _div(x, n):
    assert n & (n - 1) == 0
    return lax.shift_right_arithmetic(x, jnp.int32(n.bit_length() - 1))


def _mod(x, n):
    assert n & (n - 1) == 0
    return x & jnp.int32(n - 1)


def _const_spec(shape):
    zeros = (0,) * len(shape)
    return pl.BlockSpec(shape, lambda *_: zeros)


def _log_sigmoid(f):
    return jnp.minimum(f, 0.0) - jnp.log1p(jnp.exp(-jnp.abs(f)))


def _proj_in_kernel(x_ref, g_ref, w_ref, wf_ref, bf_ref, gmg_ref, *refs, positions_minor):
    if positions_minor:
        h_ref, q_ref, kbt_ref, vbt_ref, kt_ref, vt_ref, lft_ref, u_ref, vn_ref, mq_ref = refs
    else:
        h_ref, q_ref, k_ref, v_ref, lf_ref, u_ref, vn_ref, mq_ref = refs
    hb = _rmsnorm(x_ref[...], g_ref[...]).astype(BF16)
    h_ref[...] = hb
    nt = (((1,), (1,)), ((), ()))

    def proj(c):
        return jnp.dot(hb, w_ref[:, c * 512:(c + 1) * 512], preferred_element_type=F32)

    q_ref[...] = (proj(0) * (FOX_HEAD_DIM ** -0.5 * LOG2E)).astype(BF16)
    if positions_minor:
        k_t = lax.dot_general(wf_ref[0:FOX_WIDTH, :], hb, nt, preferred_element_type=F32)
        kt_ref[...] = k_t
        kbt_ref[...] = k_t.astype(BF16)
        v_t = lax.dot_general(wf_ref[FOX_WIDTH:2 * FOX_WIDTH, :], hb, nt, preferred_element_type=F32)
        vt_ref[...] = v_t
        vbt_ref[...] = v_t.astype(BF16)
        f_t = lax.dot_general(wf_ref[2 * FOX_WIDTH:, :], hb, nt, preferred_element_type=F32) + bf_ref[...]
        lft_ref[...] = _log_sigmoid(f_t)
    else:
        k_ref[...] = proj(1)
        v_ref[...] = proj(2)
        lf_ref[...] = _log_sigmoid(jnp.dot(hb, wf_ref[...], preferred_element_type=F32) + bf_ref[...])
    u_ref[...] = _gelu(proj(3))
    vf = _gelu(proj(4))
    mu = jnp.mean(vf, axis=-1, keepdims=True)
    var = jnp.mean(jnp.square(vf - mu), axis=-1, keepdims=True)
    vn_ref[...] = (vf - mu) * lax.rsqrt(var + EPS) * gmg_ref[...]
    mq_ref[...] = proj(5)


def _proj_in(x, g, w_main, w_f, b_f, gm_g, tile0, ntiles, seq=None):
    d = x.shape[1]
    tm = TOKEN_TILE
    n = ntiles * tm
    row = lambda width: pl.BlockSpec((tm, width), lambda i: (i, 0))
    act = lambda width, dtype: jax.ShapeDtypeStruct((n, width), dtype)
    common_shapes = (act(GM_WIDTH, F32), act(GM_WIDTH, F32), act(MEM_WIDTH, F32))
    common_specs = (row(GM_WIDTH), row(GM_WIDTH), row(MEM_WIDTH))
    if seq is None:
        out_shapes = (act(d, BF16), act(FOX_WIDTH, BF16), act(FOX_WIDTH, F32), act(FOX_WIDTH, F32),
                      act(LANES, F32)) + common_shapes
        out_specs = (row(d), row(FOX_WIDTH), row(FOX_WIDTH), row(FOX_WIDTH), row(LANES)) + common_specs
    else:
        per_seq = seq // tm
        batch = n // seq
        t_spec = lambda ch: pl.BlockSpec((None, ch, tm), lambda i: (i // per_seq, 0, i % per_seq))
        t_shape = lambda ch, dtype: jax.ShapeDtypeStruct((batch, ch, seq), dtype)
        out_shapes = (act(d, BF16), act(FOX_WIDTH, BF16), t_shape(FOX_WIDTH, BF16), t_shape(FOX_WIDTH, BF16),
                      t_shape(FOX_WIDTH, F32), t_shape(FOX_WIDTH, F32), t_shape(LANES, F32)) + common_shapes
        out_specs = (row(d), row(FOX_WIDTH), t_spec(FOX_WIDTH), t_spec(FOX_WIDTH), t_spec(FOX_WIDTH),
                     t_spec(FOX_WIDTH), t_spec(LANES)) + common_specs
    return pl.pallas_call(
        functools.partial(_proj_in_kernel, positions_minor=seq is not None),
        grid=(ntiles,),
        in_specs=[pl.BlockSpec((tm, d), lambda i: (tile0 + i, 0)), _const_spec((1, d)), _const_spec(w_main.shape),
                  _const_spec(w_f.shape), _const_spec(b_f.shape), _const_spec((1, GM_WIDTH))],
        out_specs=out_specs,
        out_shape=out_shapes,
        compiler_params=_params("parallel"),
        name="proj_in_prompt" if seq is not None else "proj_in_sample",
    )(x, g, w_main, w_f, b_f, gm_g)


def _cumsum_kernel(lf_ref, o_ref, parts_ref, carry_ref):
    @pl.when(pl.program_id(1) == 0)
    def _():
        carry_ref[...] = jnp.zeros_like(carry_ref)

    tc = lf_ref.shape[-1]
    upper = (lax.broadcasted_iota(jnp.int32, (tc, tc), 0)
             <= lax.broadcasted_iota(jnp.int32, (tc, tc), 1)).astype(F32)
    y = jnp.dot(lf_ref[...], upper, precision=HIGHEST, preferred_element_type=F32) + carry_ref[...]
    carry_ref[...] = y[:, tc - 1:tc]
    y = y * LOG2E
    o_ref[...] = y
    hi = y.astype(BF16).astype(F32)
    mid = (y - hi).astype(BF16).astype(F32)
    parts_ref[0] = hi
    parts_ref[1] = mid
    parts_ref[2] = ((y - hi) - mid).astype(BF16).astype(F32)


def _cumsum_lanes(lf_t):
    b, h, s = lf_t.shape
    tc = min(CUMSUM_TILE, s)
    return pl.pallas_call(
        _cumsum_kernel,
        grid=(b, s // tc),
        in_specs=[pl.BlockSpec((None, h, tc), lambda i, j: (i, 0, j))],
        out_specs=(pl.BlockSpec((None, h, tc), lambda i, j: (i, 0, j)),
                   pl.BlockSpec((None, 3, h, tc), lambda i, j: (i, 0, 0, j))),
        out_shape=(jax.ShapeDtypeStruct((b, h, s), F32), jax.ShapeDtypeStruct((b, 3, h, s), F32)),
        scratch_shapes=[pltpu.VMEM((h, 1), F32)],
        compiler_params=_params("parallel", "arbitrary"),
        name="fox_cumsum",
    )(lf_t)


def _fox_attn_kernel(qi_tab, kj_tab, q_ref, k_ref, fa_ref, v_ref, fr_ref, o_ref,
                     qm_ref, fr_rep, m_ref, l_ref, alpha_ref, shift_ref, tmax_ref, acc_ref, s_ref, p_ref,
                     *, tq, tk):
    p = pl.program_id(2)
    qi = qi_tab[p]
    kj = kj_tab[p]
    lane = lax.broadcasted_iota(jnp.int32, (tq, LANES), 1)
    low = lane < FOX_HEAD_DIM
    n_col = tk // LANES
    n_chunk = tq // ATTN_ROWS

    @pl.when(kj == 0)
    def _():
        q = q_ref[...]
        for hh in range(2):
            qm_ref[hh, :, :LANES] = jnp.where(low if hh == 0 else jnp.logical_not(low), q, jnp.zeros_like(q))
            minus_one = (lane >= 3 * hh) & (lane < 3 * hh + 3)
            qm_ref[hh, :, LANES:] = jnp.where(minus_one, -1.0, 0.0).astype(BF16)
            fr_rep[hh] = jnp.broadcast_to(fr_ref[:, hh:hh + 1], (tq, LANES))
        m_ref[...] = jnp.full_like(m_ref, -jnp.inf)
        l_ref[...] = jnp.zeros_like(l_ref)
        acc_ref[...] = jnp.zeros_like(acc_ref)

    def step(masked):
        sub = lax.broadcasted_iota(jnp.int32, (ATTN_ROWS, LANES), 0)
        lane_r = lax.broadcasted_iota(jnp.int32, (ATTN_ROWS, LANES), 1)
        ka_t = jnp.concatenate([k_ref[...], fa_ref[...]], axis=0)
        for hh in range(2):
            s_ref[hh] = jnp.dot(qm_ref[hh], ka_t, preferred_element_type=F32)

        def scores(hh, r0):
            ts = []
            for j in range(n_col):
                t = s_ref[hh, pl.ds(r0, ATTN_ROWS), j * LANES:(j + 1) * LANES]
                if masked:
                    keep = (kj * tk + j * LANES + lane_r) <= (qi * tq + r0 + sub)
                    t = jnp.where(keep, t, -jnp.inf)
                ts.append(t)
            return ts

        for hh in range(2):
            def max_body(i, carry, hh=hh):
                r0 = pl.multiple_of(i * ATTN_ROWS, ATTN_ROWS)
                tmax_ref[hh, pl.ds(r0, ATTN_ROWS), :] = functools.reduce(jnp.maximum, scores(hh, r0))
                return carry

            lax.fori_loop(0, n_chunk, max_body, 0, unroll=ATTN_MAX_UNROLL)

        for hh in range(2):
            f_row = fr_rep[hh]
            m_prev = m_ref[hh]
            m_new = jnp.maximum(m_prev, jnp.max(tmax_ref[hh], axis=1, keepdims=True) + f_row)
            m_ref[hh] = m_new
            alpha_ref[hh] = jnp.exp2(m_prev - m_new)
            shift_ref[hh] = m_new - f_row

        v_t = v_ref[...]
        pvs = []
        for hh in range(2):
            def exp_body(i, carry, hh=hh):
                r0 = pl.multiple_of(i * ATTN_ROWS, ATTN_ROWS)
                rows = pl.ds(r0, ATTN_ROWS)
                shift = shift_ref[hh, rows, :]
                prs = [jnp.exp2(t - shift) for t in scores(hh, r0)]
                l_ref[hh, rows, :] = alpha_ref[hh, rows, :] * l_ref[hh, rows, :] + functools.reduce(jnp.add, prs)
                for j in range(n_col):
                    p_ref[hh, rows, j * LANES:(j + 1) * LANES] = prs[j].astype(BF16)
                return carry

            lax.fori_loop(0, n_chunk, exp_body, 0, unroll=ATTN_EXP_UNROLL)
            pvs.append(lax.dot_general(p_ref[hh], v_t, (((1,), (1,)), ((), ())), preferred_element_type=F32))
        acc = acc_ref[...]
        acc_ref[...] = jnp.where(low, alpha_ref[0] * acc + pvs[0], alpha_ref[1] * acc + pvs[1])

    on_diagonal = (kj + 1) * tk > qi * tq + 1
    pl.when(on_diagonal)(lambda: step(True))
    pl.when(jnp.logical_not(on_diagonal))(lambda: step(False))

    @pl.when((kj + 1) * tk >= (qi + 1) * tq)
    def _():
        l0 = jnp.sum(l_ref[0], axis=1, keepdims=True)
        l1 = jnp.sum(l_ref[1], axis=1, keepdims=True)
        o_ref[...] = (acc_ref[...] * jnp.where(low, 1.0 / l0, 1.0 / l1)).astype(BF16)


def _fox_prompt(q, kb_t, f_aug_t, vb_t, f_row, batch, seq):
    tq = min(ATTN_TQ, seq)
    tk = min(ATTN_TK, tq)
    nq = seq // tq
    pairs = [(i, j) for i in range(nq) for j in range(((i + 1) * tq) // tk)]
    qi_tab = jnp.asarray(np.array([a for a, _ in pairs], np.int32))
    kj_tab = jnp.asarray(np.array([c for _, c in pairs], np.int32))
    nqb = seq // tq
    grid_spec = pltpu.PrefetchScalarGridSpec(
        num_scalar_prefetch=2,
        grid=(batch, FOX_HEADS // 2, len(pairs)),
        in_specs=[
            pl.BlockSpec((tq, LANES), lambda b, h, p, qt, kt: (b * nqb + qt[p], h)),
            pl.BlockSpec((None, LANES, tk), lambda b, h, p, qt, kt: (b, h, kt[p])),
            pl.BlockSpec((None, None, LANES, tk), lambda b, h, p, qt, kt: (b, h, 0, kt[p])),
            pl.BlockSpec((None, LANES, tk), lambda b, h, p, qt, kt: (b, h, kt[p])),
            pl.BlockSpec((None, None, tq, 2), lambda b, h, p, qt, kt: (b, h, qt[p], 0)),
        ],
        out_specs=pl.BlockSpec((tq, LANES), lambda b, h, p, qt, kt: (b * nqb + qt[p], h)),
        scratch_shapes=[pltpu.VMEM((2, tq, 2 * LANES), BF16),
                        pltpu.VMEM((2, tq, LANES), F32),
                        pltpu.VMEM((2, tq, LANES), F32),
                        pltpu.VMEM((2, tq, LANES), F32),
                        pltpu.VMEM((2, tq, LANES), F32),
                        pltpu.VMEM((2, tq, LANES), F32),
                        pltpu.VMEM((2, tq, LANES), F32),
                        pltpu.VMEM((tq, LANES), F32),
                        pltpu.VMEM((2, tq, tk), F32),
                        pltpu.VMEM((2, tq, tk), BF16)],
    )
    return pl.pallas_call(
        functools.partial(_fox_attn_kernel, tq=tq, tk=tk),
        grid_spec=grid_spec,
        out_shape=jax.ShapeDtypeStruct((batch * seq, FOX_WIDTH), BF16),
        compiler_params=_params("parallel", "parallel", "arbitrary"),
        name="fox_prompt_attn",
    )(qi_tab, kj_tab, q, kb_t, f_aug_t, vb_t, f_row)


def _fox_sample_kernel(pt_ref, q_ref, kn_ref, vn_ref, lfn_ref, *refs, n_pages, t_new):
    k_refs = refs[:n_pages]
    v_refs = refs[n_pages:2 * n_pages]
    lf_refs = refs[2 * n_pages:3 * n_pages]
    o_ref = refs[3 * n_pages]
    t_ref = refs[3 * n_pages + 1]
    rows = FOX_HEADS * t_new
    nt = (((1,), (1,)), ((), ()))

    q = q_ref[...]
    qt = jnp.concatenate([q] * FOX_HEADS, axis=0)
    r_w = lax.broadcasted_iota(jnp.int32, (rows, FOX_WIDTH), 0)
    c_w = lax.broadcasted_iota(jnp.int32, (rows, FOX_WIDTH), 1)
    head_match = _div(r_w, t_new) == _div(c_w, FOX_HEAD_DIM)
    q_bd = jnp.where(head_match, qt, 0.0).astype(BF16)
    upper = (lax.broadcasted_iota(jnp.int32, (PAGE_SIZE, PAGE_SIZE), 0)
             <= lax.broadcasted_iota(jnp.int32, (PAGE_SIZE, PAGE_SIZE), 1)).astype(F32)
    r_p = lax.broadcasted_iota(jnp.int32, (rows, PAGE_SIZE), 0)
    c_p = lax.broadcasted_iota(jnp.int32, (rows, PAGE_SIZE), 1)
    pad = jnp.zeros((PAGE_SIZE - t_new, FOX_WIDTH), F32)

    carry = jnp.zeros((FOX_HEADS, 1), F32)
    m = jnp.full((rows, 1), -jnp.inf, F32)
    f_row = None
    for pg in range(n_pages + 1):
        if pg < n_pages:
            k_t = k_refs[pg][...].reshape(FOX_WIDTH, PAGE_SIZE).astype(BF16)
            s = jnp.dot(q_bd, k_t, preferred_element_type=F32)
            lf_t = lf_refs[pg][...]
        else:
            k_new = jnp.concatenate([kn_ref[...], pad], axis=0).astype(BF16)
            s = lax.dot_general(q_bd, k_new, nt, preferred_element_type=F32)
            lf_t = lfn_ref[...]
        f_t = jnp.dot(lf_t, upper, precision=HIGHEST, preferred_element_type=F32) + carry
        carry = f_t[:, PAGE_SIZE - 1:PAGE_SIZE]
        f_t = f_t * LOG2E
        f_col = jnp.concatenate(
            [jnp.broadcast_to(f_t[h:h + 1, :], (t_new, PAGE_SIZE)) for h in range(FOX_HEADS)], axis=0)
        t = s - f_col
        if pg == n_pages:
            t = jnp.where(c_p <= _mod(r_p, t_new), t, -jnp.inf)
            f_row = jnp.sum(jnp.where(c_p == _mod(r_p, t_new), f_col, 0.0), axis=1, keepdims=True)
        t_ref[pg] = t
        m = jnp.maximum(m, jnp.max(t, axis=1, keepdims=True))
    m_logit = m + f_row
    shift = m_logit - f_row
    l = jnp.zeros((rows, 1), F32)
    acc = jnp.zeros((rows, FOX_WIDTH), F32)
    for pg in range(n_pages + 1):
        pr = jnp.exp2(t_ref[pg] - shift)
        l = l + jnp.sum(pr, axis=1, keepdims=True)
        if pg < n_pages:
            v_t = v_refs[pg][...].reshape(FOX_WIDTH, PAGE_SIZE).astype(BF16)
            acc = acc + lax.dot_general(pr.astype(BF16), v_t, nt, preferred_element_type=F32)
        else:
            v_new = jnp.concatenate([vn_ref[...], pad], axis=0).astype(BF16)
            acc = acc + jnp.dot(pr.astype(BF16), v_new, preferred_element_type=F32)
    o = acc / l
    c_o = lax.broadcasted_iota(jnp.int32, (t_new, FOX_WIDTH), 1)
    out = jnp.zeros((t_new, FOX_WIDTH), F32)
    for h in range(FOX_HEADS):
        out = jnp.where(_div(c_o, FOX_HEAD_DIM) == h, o[h * t_new:(h + 1) * t_new, :], out)
    o_ref[...] = out


def _fox_sample(page_table, q32, k32, v32, lf_new_t, cache_kt, cache_vt, cache_lf_t, layer, row0, t_new):
    nb, n_pages = page_table.shape
    blk0 = row0 // t_new
    tok = pl.BlockSpec((t_new, FOX_WIDTH), lambda b, pt: (blk0 + b, 0))

    def kv_spec(pg):
        return pl.BlockSpec((None, None, FOX_HEADS, FOX_HEAD_DIM, PAGE_SIZE),
                            lambda b, pt: (layer, pt[b, pg], 0, 0, 0))

    def lf_spec(pg):
        return pl.BlockSpec((None, None, FOX_HEADS, PAGE_SIZE), lambda b, pt: (layer, pt[b, pg], 0, 0))

    in_specs = [tok, tok, tok, pl.BlockSpec((None, FOX_HEADS, LANES), lambda b, pt: (b, 0, 0))]
    in_specs += [kv_spec(pg) for pg in range(n_pages)] * 2
    in_specs += [lf_spec(pg) for pg in range(n_pages)]
    grid_spec = pltpu.PrefetchScalarGridSpec(
        num_scalar_prefetch=1,
        grid=(nb,),
        in_specs=in_specs,
        out_specs=pl.BlockSpec((t_new, FOX_WIDTH), lambda b, pt: (b, 0)),
        scratch_shapes=[pltpu.VMEM((n_pages + 1, FOX_HEADS * t_new, PAGE_SIZE), F32)],
    )
    return pl.pallas_call(
        functools.partial(_fox_sample_kernel, n_pages=n_pages, t_new=t_new),
        grid_spec=grid_spec,
        out_shape=jax.ShapeDtypeStruct((nb * t_new, FOX_WIDTH), F32),
        compiler_params=_params("parallel"),
        name="fox_sample_attn",
    )(page_table, q32, k32, v32, lf_new_t, *([cache_kt] * n_pages), *([cache_vt] * n_pages),
      *([cache_lf_t] * n_pages))


def _norm_matmul_kernel(x_ref, g_ref, w_ref, o_ref):
    hb = _rmsnorm(x_ref[...], g_ref[...]).astype(BF16)
    o_ref[...] = jnp.dot(hb, w_ref[...], preferred_element_type=F32)


def _norm_matmul(x, g, w):
    n, d = x.shape
    tm = min(TOKEN_TILE, n)
    return pl.pallas_call(
        _norm_matmul_kernel,
        grid=(n // tm,),
        in_specs=[pl.BlockSpec((tm, d), lambda i: (i, 0)), _const_spec((1, d)), _const_spec(w.shape)],
        out_specs=pl.BlockSpec((tm, w.shape[1]), lambda i: (i, 0)),
        out_shape=jax.ShapeDtypeStruct((n, w.shape[1]), F32),
        compiler_params=_params("parallel"),
        name="mem_kv",
    )(x, g, w)


def _mem_attend_kernel(q_ref, mk_ref, mv_ref, o_ref):
    scale = MEM_HEAD_DIM ** -0.5
    n_mem = mk_ref.shape[0]
    t = q_ref.shape[0] // n_mem
    for g in range(n_mem):
        rs = slice(g * t, (g + 1) * t)
        for h in range(MEM_HEADS):
            sl = slice(h * MEM_HEAD_DIM, (h + 1) * MEM_HEAD_DIM)
            s = lax.dot_general(q_ref[rs, sl].astype(BF16), mk_ref[g, :, sl].astype(BF16),
                                (((1,), (1,)), ((), ())), preferred_element_type=F32) * scale
            e = jnp.exp(s - jnp.max(s, axis=1, keepdims=True))
            pr = e / jnp.sum(e, axis=1, keepdims=True)
            o_ref[rs, sl] = jnp.dot(pr.astype(BF16), mv_ref[g, :, sl].astype(BF16), preferred_element_type=F32)


def _mem_attend_few_rows_kernel(q_ref, mk_ref, mv_ref, o_ref):
    scale = MEM_HEAD_DIM ** -0.5
    n_mem, m_len = mk_ref.shape[0], mk_ref.shape[1]
    t = q_ref.shape[0] // n_mem
    rows, width = MEM_HEADS * t, m_len * MEM_HEADS
    r = lax.broadcasted_iota(jnp.int32, (rows, width), 0)
    c = lax.broadcasted_iota(jnp.int32, (rows, width), 1)
    valid = _mod(c, MEM_HEADS) == _div(r, t)
    for g in range(n_mem):
        rs = slice(g * t, (g + 1) * t)
        q_ht = jnp.concatenate([q_ref[rs, h * MEM_HEAD_DIM:(h + 1) * MEM_HEAD_DIM] for h in range(MEM_HEADS)],
                               axis=0).astype(BF16)
        mk2 = mk_ref[g].reshape(width, MEM_HEAD_DIM).astype(BF16)
        s = lax.dot_general(q_ht, mk2, (((1,), (1,)), ((), ())), preferred_element_type=F32) * scale
        s = jnp.where(valid, s, -jnp.inf)
        e = jnp.exp(s - jnp.max(s, axis=1, keepdims=True))
        pr = e / jnp.sum(e, axis=1, keepdims=True)
        o = jnp.dot(pr.astype(BF16), mv_ref[g].reshape(width, MEM_HEAD_DIM).astype(BF16),
                    preferred_element_type=F32)
        for h in range(MEM_HEADS):
            o_ref[rs, h * MEM_HEAD_DIM:(h + 1) * MEM_HEAD_DIM] = o[h * t:(h + 1) * t, :]


def _mem_attend(mq, mk, mv, row0, n_mem, t_per_mem, name, layer=None):
    tm = min(TOKEN_TILE * 2, t_per_mem)
    tiles = t_per_mem // tm
    group = max(1, min(MEM_GROUP_ROWS // tm, n_mem)) if tiles == 1 else 1
    assert n_mem % group == 0
    rows = tm * group
    blk0 = row0 // rows
    if layer is None:
        assert group == 1
        mem_spec = pl.BlockSpec((1, mk.shape[1], MEM_WIDTH), lambda b, i: (b, 0, 0))
    else:
        assert group > 1
        mem_spec = pl.BlockSpec((None, group, mk.shape[2], MEM_HEADS, MEM_HEAD_DIM),
                                lambda b, i: (layer, b, 0, 0, 0))
    return pl.pallas_call(
        _mem_attend_few_rows_kernel if group > 1 else _mem_attend_kernel,
        grid=(n_mem // group, tiles),
        in_specs=[pl.BlockSpec((rows, MEM_WIDTH), lambda b, i: (blk0 + b * tiles + i, 0)), mem_spec, mem_spec],
        out_specs=pl.BlockSpec((rows, MEM_WIDTH), lambda b, i: (b * tiles + i, 0)),
        out_shape=jax.ShapeDtypeStruct((n_mem * t_per_mem, MEM_WIDTH), F32),
        compiler_params=_params("parallel", "parallel"),
        name=name,
    )(mq, mk, mv)


def _merge_kernel(x_ref, h_ref, fox_ref, u_ref, vn_ref, mem_ref, ws_ref, bs_ref,
                  wg_ref, bg_ref, wbr_ref, wo_ref, o_ref, gm_ref, *, chunk):
    tm, d = x_ref.shape
    r_c = lax.broadcasted_iota(jnp.int32, (CHUNK, CHUNK), 0)
    c_c = lax.broadcasted_iota(jnp.int32, (CHUNK, CHUNK), 1)
    keep = (c_c <= r_c) & (_div(r_c, chunk) == _div(c_c, chunk))
    for g in range(GM_GROUPS):
        w = jnp.where(keep, ws_ref[g], 0.0).astype(BF16)
        sl = slice(g * GM_GROUP_DIM, (g + 1) * GM_GROUP_DIM)
        for c in range(tm // CHUNK):
            rs = slice(c * CHUNK, (c + 1) * CHUNK)
            mix = jnp.dot(w, vn_ref[rs, sl].astype(BF16), preferred_element_type=F32) + bs_ref[:, g:g + 1]
            gm_ref[rs, sl] = (u_ref[rs, sl] * mix).astype(BF16)

    hb = h_ref[...]
    branches = (fox_ref[...].astype(BF16), gm_ref[...], mem_ref[...].astype(BF16))
    merged = None
    for r in range(N_BRANCH):
        gate = jax.nn.sigmoid(jnp.dot(hb, wg_ref[:, r * d:(r + 1) * d], preferred_element_type=F32)
                              + bg_ref[:, r * d:(r + 1) * d])
        term = gate * jnp.dot(branches[r], wbr_ref[r], preferred_element_type=F32)
        merged = term if merged is None else merged + term
    o_ref[...] = x_ref[...] + jnp.dot(merged.astype(BF16), wo_ref[...], preferred_element_type=F32)


def _merge(x, x_tile0, h, fox_o, u, vn, mem_o, ws, bs_t, w_gate, b_gate, w_br, w_o, ntiles, chunk, name):
    d = x.shape[1]
    tm = TOKEN_TILE
    own = lambda width: pl.BlockSpec((tm, width), lambda i: (i, 0))
    in_specs = [pl.BlockSpec((tm, d), lambda i: (x_tile0 + i, 0)), own(d), own(FOX_WIDTH), own(GM_WIDTH),
                own(GM_WIDTH), own(MEM_WIDTH), _const_spec(ws.shape), _const_spec(bs_t.shape),
                _const_spec(w_gate.shape), _const_spec((1, N_BRANCH * d)), _const_spec(w_br.shape),
                _const_spec(w_o.shape)]
    return pl.pallas_call(
        functools.partial(_merge_kernel, chunk=chunk),
        grid=(ntiles,),
        in_specs=in_specs,
        out_specs=pl.BlockSpec((tm, d), lambda i: (i, 0)),
        out_shape=jax.ShapeDtypeStruct((ntiles * tm, d), F32),
        scratch_shapes=[pltpu.VMEM((tm, GM_WIDTH), BF16)],
        compiler_params=_params("parallel"),
        name=name,
    )(x, h, fox_o, u, vn, mem_o, ws, bs_t, w_gate, b_gate, w_br, w_o)


def _moe_route_kernel(x_ref, g_ref, wr_ref, br_ref, hn_ref, meta_ref, cnt_ref, carry_ref):
    @pl.when(pl.program_id(0) == 0)
    def _():
        carry_ref[...] = jnp.zeros_like(carry_ref)

    tm = x_ref.shape[0]
    hn = _rmsnorm(x_ref[...], g_ref[...])
    hn_ref[...] = hn
    logits = jnp.dot(hn, wr_ref[...], precision=HIGHEST, preferred_element_type=F32) + br_ref[...]
    col = lax.broadcasted_iota(jnp.int32, (tm, LANES), 1)
    colf = col.astype(F32)
    big = jnp.float32(1e9)

    def masked_softmax(mask):
        z = jnp.where(mask, logits, -jnp.inf)
        e = jnp.exp(z - jnp.max(z, axis=1, keepdims=True))
        return e / jnp.sum(e, axis=1, keepdims=True)

    def first_argmax(vals, vmax, mask):
        return jnp.min(jnp.where(mask & (vals == vmax), colf, big), axis=1, keepdims=True)

    is_group = col < N_GROUPS
    g_prob = masked_softmax(is_group)
    p_g = jnp.max(g_prob, axis=1, keepdims=True)
    g_idx = first_argmax(g_prob, p_g, is_group)
    e_col = col - N_GROUPS
    in_group = (e_col >= 0) & (e_col < N_EXPERTS) & (_div(e_col, EXPERTS_PER_GROUP).astype(F32) == g_idx)
    e_prob = masked_softmax(in_group)
    p1 = jnp.max(e_prob, axis=1, keepdims=True)
    i1 = first_argmax(e_prob, p1, in_group)
    rest = in_group & (colf != i1)
    p2 = jnp.max(jnp.where(rest, e_prob, -1.0), axis=1, keepdims=True)
    i2 = first_argmax(e_prob, p2, rest)
    denom = p1 + p2
    w1 = p1 / denom * p_g
    w2 = p2 / denom * p_g
    e1 = i1 - N_GROUPS
    e2 = i2 - N_GROUPS

    onehot = ((colf == e1) | (colf == e2)).astype(F32)
    lower = (lax.broadcasted_iota(jnp.int32, (tm, tm), 1)
             < lax.broadcasted_iota(jnp.int32, (tm, tm), 0)).astype(BF16)
    before = jnp.dot(lower, onehot.astype(BF16), preferred_element_type=F32) + carry_ref[...]
    r1 = jnp.sum(jnp.where(colf == e1, before, 0.0), axis=1, keepdims=True)
    r2 = jnp.sum(jnp.where(colf == e2, before, 0.0), axis=1, keepdims=True)
    carry_ref[...] = carry_ref[...] + jnp.sum(onehot, axis=0, keepdims=True)
    cnt_ref[...] = jnp.broadcast_to(carry_ref[...], cnt_ref.shape)

    meta = jnp.zeros((tm, LANES), F32)
    for lane_id, val in enumerate((e1, e2, r1, r2, w1, w2)):
        meta = jnp.where(col == lane_id, val, meta)
    meta_ref[...] = meta


def _moe_route(x, g, w_route, b_route):
    n, d = x.shape
    tm = ROUTE_TILE
    return pl.pallas_call(
        _moe_route_kernel,
        grid=(n // tm,),
        in_specs=[pl.BlockSpec((tm, d), lambda i: (i, 0)), _const_spec((1, d)),
                  _const_spec(w_route.shape), _const_spec((1, LANES))],
        out_specs=(pl.BlockSpec((tm, d), lambda i: (i, 0)), pl.BlockSpec((tm, LANES), lambda i: (i, 0)),
                   _const_spec((8, LANES))),
        out_shape=(jax.ShapeDtypeStruct((n, d), F32), jax.ShapeDtypeStruct((n, LANES), F32),
                   jax.ShapeDtypeStruct((8, LANES), F32)),
        scratch_shapes=[pltpu.VMEM((1, LANES), F32)],
        compiler_params=_params("arbitrary"),
        name="moe_route",
    )(x, g, w_route, b_route)


def _moe_expert_kernel(te_ref, tv_ref, src_ref, src_next_ref, hn_hbm, wgu_ref, wdn_ref, ys_ref,
                       x_buf, wgu_bf, wdn_bf, sem, *, tg):
    t = pl.program_id(0)
    n_t = pl.num_programs(0)
    slot = t % 2

    def gather(idx_ref, buf_slot):
        def issue(i, c):
            base = pl.multiple_of(i * DMA_UNROLL, DMA_UNROLL)
            for k in range(DMA_UNROLL):
                pltpu.make_async_copy(hn_hbm.at[pl.ds(idx_ref[0, base + k], 1)],
                                      x_buf.at[buf_slot, pl.ds(base + k, 1)], sem.at[buf_slot]).start()
            return c

        lax.fori_loop(0, tg // DMA_UNROLL, issue, 0)

    @pl.when((t == 0) & (tv_ref[0] == 1))
    def _():
        gather(src_ref, 0)

    @pl.when((t + 1 < n_t) & (tv_ref[jnp.minimum(t + 1, n_t - 1)] == 1))
    def _():
        gather(src_next_ref, 1 - slot)

    @pl.when(tv_ref[t] == 1)
    def _():
        @pl.when((t == 0) | (te_ref[t] != te_ref[jnp.maximum(t - 1, 0)]))
        def _():
            wgu_bf[...] = wgu_ref[...].astype(BF16)
            wdn_bf[...] = wdn_ref[...].astype(BF16)

        pltpu.make_async_copy(hn_hbm.at[pl.ds(0, tg)], x_buf.at[slot], sem.at[slot]).wait()
        gu = jnp.dot(x_buf[slot].astype(BF16), wgu_bf[...], preferred_element_type=F32)
        gate = gu[:, :EXPERT_FF]
        act = gate * jax.nn.sigmoid(gate) * gu[:, EXPERT_FF:]
        ys_ref[...] = jnp.dot(act.astype(BF16), wdn_bf[...], preferred_element_type=F32)

    @pl.when(tv_ref[t] != 1)
    def _():
        ys_ref[...] = jnp.zeros_like(ys_ref)


def _moe_experts(tile_expert, tile_valid, src_idx, hn, w_gu, w_dn, layer):
    n_tiles = tile_expert.shape[0]
    tg = MOE_TILE
    d = hn.shape[1]
    grid_spec = pltpu.PrefetchScalarGridSpec(
        num_scalar_prefetch=2,
        grid=(n_tiles,),
        in_specs=[
            pl.BlockSpec((None, 1, tg), lambda t, te, tv: (t, 0, 0), memory_space=pltpu.SMEM),
            pl.BlockSpec((None, 1, tg), lambda t, te, tv: (jnp.minimum(t + 1, n_tiles - 1), 0, 0),
                         memory_space=pltpu.SMEM),
            pl.BlockSpec(memory_space=pl.ANY),
            pl.BlockSpec((None, None, d, 2 * EXPERT_FF), lambda t, te, tv: (layer, te[t], 0, 0)),
            pl.BlockSpec((None, None, EXPERT_FF, d), lambda t, te, tv: (layer, te[t], 0, 0)),
        ],
        out_specs=pl.BlockSpec((tg, d), lambda t, te, tv: (t, 0)),
        scratch_shapes=[pltpu.VMEM((2, tg, d), F32), pltpu.VMEM((d, 2 * EXPERT_FF), BF16),
                        pltpu.VMEM((EXPERT_FF, d), BF16), pltpu.SemaphoreType.DMA((2,))],
    )
    return pl.pallas_call(
        functools.partial(_moe_expert_kernel, tg=tg),
        grid_spec=grid_spec,
        out_shape=jax.ShapeDtypeStruct((n_tiles * tg, d), F32),
        compiler_params=_params("arbitrary"),
        name="moe_experts",
    )(tile_expert, tile_valid, src_idx, src_idx, hn, w_gu, w_dn)


def _moe_combine_kernel(dest_ref, x_ref, meta_ref, fg_ref, ys_hbm, o_ref, buf, sem, *, tm, final):
    def issue(i, c):
        base = pl.multiple_of(i * DMA_UNROLL, DMA_UNROLL)
        for k in range(DMA_UNROLL):
            for slot in range(2):
                pltpu.make_async_copy(ys_hbm.at[pl.ds(dest_ref[0, 2 * (base + k) + slot], 1)],
                                      buf.at[slot, pl.ds(base + k, 1)], sem.at[slot]).start()
        return c

    lax.fori_loop(0, tm // DMA_UNROLL, issue, 0)
    for slot in range(2):
        pltpu.make_async_copy(ys_hbm.at[pl.ds(0, tm)], buf.at[slot], sem.at[slot]).wait()
    meta = meta_ref[...]
    y = x_ref[...] + (meta[:, 4:5] * buf[0] + meta[:, 5:6] * buf[1])
    if final:
        y = _rmsnorm(y, fg_ref[...])
    o_ref[...] = y


def _moe_combine(dest, x, meta, final_g, ys, final, tile0, ntiles):
    d = x.shape[1]
    tm = TOKEN_TILE
    return pl.pallas_call(
        functools.partial(_moe_combine_kernel, tm=tm, final=final),
        grid=(ntiles,),
        in_specs=[pl.BlockSpec((None, 1, 2 * tm), lambda i: (tile0 + i, 0, 0), memory_space=pltpu.SMEM),
                  pl.BlockSpec((tm, d), lambda i: (tile0 + i, 0)),
                  pl.BlockSpec((tm, LANES), lambda i: (tile0 + i, 0)),
                  _const_spec((1, d)), pl.BlockSpec(memory_space=pl.ANY)],
        out_specs=pl.BlockSpec((tm, d), lambda i: (i, 0)),
        out_shape=jax.ShapeDtypeStruct((ntiles * tm, d), F32),
        scratch_shapes=[pltpu.VMEM((2, tm, d), F32), pltpu.SemaphoreType.DMA((2,))],
        compiler_params=_params("arbitrary"),
        name="moe_combine",
    )(dest, x, meta, final_g, ys)


def _moe_layer(x, g, w_route, b_route, w_gu, w_dn, layer, final_g, final, splits):
    n, d = x.shape
    tg = MOE_TILE
    hn, meta, counts = _moe_route(x, g, w_route, b_route)
    counts = counts[0, :N_EXPERTS].astype(jnp.int32)
    padded = ((counts + tg - 1) // tg) * tg
    ends = jnp.cumsum(padded)
    starts = ends - padded
    eid = meta[:, 0:2].astype(jnp.int32)
    rank = meta[:, 2:4].astype(jnp.int32)
    dest = starts[eid] + rank
    n_tiles = (2 * n) // tg + N_EXPERTS
    tile_start = jnp.arange(n_tiles, dtype=jnp.int32) * tg
    tile_expert = jnp.minimum(jnp.sum((ends[None, :] <= tile_start[:, None]).astype(jnp.int32), axis=1),
                              N_EXPERTS - 1)
    tile_valid = (tile_start < ends[-1]).astype(jnp.int32)
    token = jnp.broadcast_to(jnp.arange(n, dtype=jnp.int32)[:, None], (n, 2))
    src_idx = jnp.zeros((n_tiles * tg,), jnp.int32).at[dest.reshape(-1)].set(token.reshape(-1))
    ys = _moe_experts(tile_expert, tile_valid, src_idx.reshape(n_tiles, 1, tg), hn, w_gu, w_dn, layer)
    dest_tiles = dest.reshape(n // TOKEN_TILE, 1, 2 * TOKEN_TILE)
    return tuple(_moe_combine(dest_tiles, x, meta, final_g, ys, final, t0, nt) for t0, nt in splits)


def kernel(x_prompt, x_sample, cache_k, cache_v, cache_logf, cache_mem_k, cache_mem_v, page_table, mem_prompt, norm1_g, w_in, b_f, gm_norm_g, gm_ws, gm_bs, mem_norm_g, w_mem_kv, w_gate, b_gate, w_br, w_o, norm2_g, w_rg, b_rg, w_re, b_re, w_moe_gu, w_moe_dn, final_g):
    batch, seq, d = x_prompt.shape
    dec_batch, dec_seq, _ = x_sample.shape
    depth = w_in.shape[0]
    n_pool = cache_k.shape[1]
    mem_len = mem_prompt.shape[1]
    n_p = batch * seq
    n_s = dec_batch * dec_seq
    assert n_p % (2 * TOKEN_TILE) == 0 and n_s % (2 * TOKEN_TILE) == 0 and seq % CHUNK == 0
    assert CHUNK % dec_seq == 0 and (n_p + n_s) % ROUTE_TILE == 0
    n_prompt_tiles = n_p // TOKEN_TILE

    n_sample_tiles = n_s // TOKEN_TILE
    n_all = n_p + n_s
    x_of = {"prompt": (x_prompt.reshape(n_p, d), 0), "sample": (x_sample.reshape(n_s, d), 0)}
    mem_flat = mem_prompt.reshape(batch * mem_len, d)
    ckt = jnp.transpose(cache_k, (0, 1, 3, 4, 2))
    cvt = jnp.transpose(cache_v, (0, 1, 3, 4, 2))
    clf_t = jnp.swapaxes(cache_logf, 2, 3)

    q_end, f_end = 3 * FOX_WIDTH, 3 * FOX_WIDTH + FOX_HEADS
    reps = CHUNK // dec_seq
    outs = {name: [] for name in ("kp", "vp", "fp", "mkp", "mvp", "ks", "vs", "fs", "gs")}
    for l in range(depth):
        w_main = jnp.concatenate([w_in[l][:, :q_end], w_in[l][:, f_end:]], axis=1).astype(BF16)
        w_f = jnp.pad(w_in[l][:, q_end:f_end], ((0, 0), (0, LANES - FOX_HEADS))).astype(BF16)
        bf_pad = jnp.pad(b_f[l], (0, LANES - FOX_HEADS)).reshape(1, LANES)
        ws_sample = jnp.tile(gm_ws[l][:, :dec_seq, :dec_seq], (1, reps, reps))
        bs_sample_t = jnp.tile(gm_bs[l][:, :dec_seq], (1, reps)).T
        w_route = jnp.pad(jnp.concatenate([w_rg[l], w_re[l]], axis=1),
                          ((0, 0), (0, LANES - N_GROUPS - N_EXPERTS)))
        b_route = jnp.pad(jnp.concatenate([b_rg[l], b_re[l]]), (0, LANES - N_GROUPS - N_EXPERTS)).reshape(1, LANES)

        w_kvf_t = jnp.concatenate([w_main[:, FOX_WIDTH:3 * FOX_WIDTH], w_f], axis=1).T
        norm_g, gm_g = norm1_g[l].reshape(1, d), gm_norm_g[l].reshape(1, GM_WIDTH)

        h_p, q_p, kb_t, vb_t, k_t, v_t, lf_t, u_p, vn_p, mq_p = _proj_in(
            x_of["prompt"][0], norm_g, w_main, w_kvf_t, bf_pad.reshape(LANES, 1), gm_g, x_of["prompt"][1],
            n_prompt_tiles, seq=seq)
        h_s, q_s, k_s, v_s, lf_s, u_s, vn_s, mq_s = _proj_in(
            x_of["sample"][0], norm_g, w_main, w_f, bf_pad, gm_g, x_of["sample"][1], n_sample_tiles)

        lf_p_t = lf_t[:, :FOX_HEADS, :]
        f_t, f_parts = _cumsum_lanes(lf_p_t)
        f_row = jnp.swapaxes(f_t.reshape(batch, FOX_HEADS // 2, 2, seq), 2, 3)
        f_aug_t = jnp.swapaxes(f_parts, 1, 2).reshape(batch, FOX_HEADS // 2, 6, seq)
        f_aug_t = jnp.pad(f_aug_t, ((0, 0), (0, 0), (0, LANES - 6), (0, 0))).astype(BF16)
        fox_p = _fox_prompt(q_p, kb_t, f_aug_t, vb_t, f_row, batch, seq)

        lf_s = lf_s[:, :FOX_HEADS].reshape(dec_batch, dec_seq, FOX_HEADS)
        lf_new_t = jnp.pad(jnp.swapaxes(lf_s, 1, 2), ((0, 0), (0, 0), (0, LANES - dec_seq)))
        fox_s = _fox_sample(page_table, q_s.astype(F32), k_s, v_s, lf_new_t, ckt, cvt, clf_t, l, 0, dec_seq)

        kv = _norm_matmul(mem_flat, mem_norm_g[l].reshape(1, d), w_mem_kv[l].astype(BF16))
        mk_p = kv[:, :MEM_WIDTH].reshape(batch, mem_len, MEM_WIDTH)
        mv_p = kv[:, MEM_WIDTH:].reshape(batch, mem_len, MEM_WIDTH)
        mem_p = _mem_attend(mq_p, mk_p, mv_p, 0, batch, seq, "mem_attend_prompt")
        mem_s = _mem_attend(mq_s, cache_mem_k, cache_mem_v, 0, dec_batch, dec_seq, "mem_attend_sample", layer=l)

        merge_w = (w_gate[l].astype(BF16), b_gate[l].reshape(1, N_BRANCH * d), w_br[l].astype(BF16),
                   w_o[l].astype(BF16))
        x_p = _merge(*x_of["prompt"], h_p, fox_p, u_p, vn_p, mem_p, gm_ws[l], gm_bs[l].T, *merge_w,
                     n_prompt_tiles, CHUNK, "merge_prompt")
        x_s = _merge(*x_of["sample"], h_s, fox_s, u_s, vn_s, mem_s, ws_sample, bs_sample_t, *merge_w,
                     n_sample_tiles, dec_seq, "merge_sample")
        x_mid = jnp.concatenate([x_p, x_s], axis=0)
        last = l == depth - 1
        splits = ((0, n_prompt_tiles), (n_prompt_tiles, n_sample_tiles)) if last else ((0, n_all // TOKEN_TILE),)
        x_out = _moe_layer(x_mid, norm2_g[l].reshape(1, d), w_route, b_route, w_moe_gu, w_moe_dn, l,
                           final_g.reshape(1, d), last, splits)
        if last:
            x_of = {"prompt": (x_out[0], 0), "sample": (x_out[1], 0)}
        else:
            x_of = {"prompt": (x_out[0], 0), "sample": (x_out[0], n_prompt_tiles)}

        to_bshd = lambda a: jnp.transpose(a.reshape(batch, FOX_HEADS, FOX_HEAD_DIM, seq), (0, 3, 1, 2))
        outs["kp"].append(to_bshd(k_t))
        outs["vp"].append(to_bshd(v_t))
        outs["fp"].append(jnp.swapaxes(lf_p_t, 1, 2))
        outs["mkp"].append(mk_p.reshape(batch, mem_len, MEM_HEADS, MEM_HEAD_DIM))
        outs["mvp"].append(mv_p.reshape(batch, mem_len, MEM_HEADS, MEM_HEAD_DIM))
        outs["ks"].append(k_s.reshape(dec_batch, dec_seq, FOX_HEADS, FOX_HEAD_DIM))
        outs["vs"].append(v_s.reshape(dec_batch, dec_seq, FOX_HEADS, FOX_HEAD_DIM))
        outs["fs"].append(lf_s)
        outs["gs"].append(vn_s.reshape(dec_batch, dec_seq, GM_WIDTH))

    y_prompt = x_of["prompt"][0].reshape(batch, seq, d)
    y_sample = x_of["sample"][0].reshape(dec_batch, dec_seq, d)
    return (y_prompt, y_sample) + tuple(
        jnp.stack(outs[name]) for name in ("kp", "vp", "fp", "mkp", "mvp", "ks", "vs", "fs", "gs"))
```

```python
import functools

import numpy as np
import jax
import jax.numpy as jnp
from jax import lax
from jax.experimental import pallas as pl
from jax.experimental.pallas import tpu as pltpu

F32 = jnp.float32
BF16 = jnp.bfloat16
HIGHEST = lax.Precision.HIGHEST

EPS = 1e-6
LOG2E = 1.4426950408889634
FOX_HEADS = 8
FOX_HEAD_DIM = 64
FOX_WIDTH = FOX_HEADS * FOX_HEAD_DIM
GM_GROUPS = 4
GM_GROUP_DIM = 128
GM_WIDTH = GM_GROUPS * GM_GROUP_DIM
CHUNK = 128
MEM_HEADS = 4
MEM_HEAD_DIM = 128
MEM_WIDTH = MEM_HEADS * MEM_HEAD_DIM
N_BRANCH = 3
N_GROUPS = 4
EXPERTS_PER_GROUP = 8
N_EXPERTS = N_GROUPS * EXPERTS_PER_GROUP
EXPERT_FF = 512
PAGE_SIZE = 128

LANES = 128
VMEM_LIMIT_BYTES = 56 * 1024 * 1024
TOKEN_TILE = 256
ATTN_TQ = 1024
ATTN_TK = 512
ATTN_ROWS = 16
ATTN_MAX_UNROLL = 64
ATTN_EXP_UNROLL = 64
CUMSUM_TILE = 512
MOE_TILE = 256
ROUTE_TILE = 512
MEM_GROUP_ROWS = 64
DMA_UNROLL = 8


def _params(*sem):
    return pltpu.CompilerParams(dimension_semantics=sem, vmem_limit_bytes=VMEM_LIMIT_BYTES)


def _rmsnorm(x, g):
    return x * lax.rsqrt(jnp.mean(x * x, axis=-1, keepdims=True) + EPS) * g


def _gelu(x):
    return 0.5 * x * (1.0 + jnp.tanh(np.sqrt(2.0 / np.pi).astype(np.float32) * (x + 0.044715 * (x * x * x))))


def _div(x, n):
    assert n & (n - 1) == 0
    return lax.shift_right_arithmetic(x, jnp.int32(n.bit_length() - 1))


def _mod(x, n):
    assert n & (n - 1) == 0
    return x & jnp.int32(n - 1)


def _const_spec(shape):
    zeros = (0,) * len(shape)
    return pl.BlockSpec(shape, lambda *_: zeros)


def _log_sigmoid(f):
    return jnp.minimum(f, 0.0) - jnp.log1p(jnp.exp(-jnp.abs(f)))


def _proj_in_kernel(x_ref, g_ref, w_ref, wf_ref, bf_ref, gmg_ref, *refs, positions_minor):
    if positions_minor:
        h_ref, q_ref, kbt_ref, vbt_ref, kt_ref, vt_ref, lft_ref, u_ref, vn_ref, mq_ref = refs
    else:
        h_ref, q_ref, k_ref, v_ref, lf_ref, u_ref, vn_ref, mq_ref = refs
    hb = _rmsnorm(x_ref[...], g_ref[...]).astype(BF16)
    h_ref[...] = hb
    nt = (((1,), (1,)), ((), ()))

    def proj(c):
        return jnp.dot(hb, w_ref[:, c * 512:(c + 1) * 512], preferred_element_type=F32)

    q_ref[...] = (proj(0) * (FOX_HEAD_DIM ** -0.5 * LOG2E)).astype(BF16)
    if positions_minor:
        k_t = lax.dot_general(wf_ref[0:FOX_WIDTH, :], hb, nt, preferred_element_type=F32)
        kt_ref[...] = k_t
        kbt_ref[...] = k_t.astype(BF16)
        v_t = lax.dot_general(wf_ref[FOX_WIDTH:2 * FOX_WIDTH, :], hb, nt, preferred_element_type=F32)
        vt_ref[...] = v_t
        vbt_ref[...] = v_t.astype(BF16)
        f_t = lax.dot_general(wf_ref[2 * FOX_WIDTH:, :], hb, nt, preferred_element_type=F32) + bf_ref[...]
        lft_ref[...] = _log_sigmoid(f_t)
    else:
        k_ref[...] = proj(1)
        v_ref[...] = proj(2)
        lf_ref[...] = _log_sigmoid(jnp.dot(hb, wf_ref[...], preferred_element_type=F32) + bf_ref[...])
    u_ref[...] = _gelu(proj(3))
    vf = _gelu(proj(4))
    mu = jnp.mean(vf, axis=-1, keepdims=True)
    var = jnp.mean(jnp.square(vf - mu), axis=-1, keepdims=True)
    vn_ref[...] = (vf - mu) * lax.rsqrt(var + EPS) * gmg_ref[...]
    mq_ref[...] = proj(5)


def _proj_in(x, g, w_main, w_f, b_f, gm_g, tile0, ntiles, seq=None):
    d = x.shape[1]
    tm = TOKEN_TILE
    n = ntiles * tm
    row = lambda width: pl.BlockSpec((tm, width), lambda i: (i, 0))
    act = lambda width, dtype: jax.ShapeDtypeStruct((n, width), dtype)
    common_shapes = (act(GM_WIDTH, F32), act(GM_WIDTH, F32), act(MEM_WIDTH, F32))
    common_specs = (row(GM_WIDTH), row(GM_WIDTH), row(MEM_WIDTH))
    if seq is None:
        out_shapes = (act(d, BF16), act(FOX_WIDTH, BF16), act(FOX_WIDTH, F32), act(FOX_WIDTH, F32),
                      act(LANES, F32)) + common_shapes
        out_specs = (row(d), row(FOX_WIDTH), row(FOX_WIDTH), row(FOX_WIDTH), row(LANES)) + common_specs
    else:
        per_seq = seq // tm
        batch = n // seq
        t_spec = lambda ch: pl.BlockSpec((None, ch, tm), lambda i: (i // per_seq, 0, i % per_seq))
        t_shape = lambda ch, dtype: jax.ShapeDtypeStruct((batch, ch, seq), dtype)
        out_shapes = (act(d, BF16), act(FOX_WIDTH, BF16), t_shape(FOX_WIDTH, BF16), t_shape(FOX_WIDTH, BF16),
                      t_shape(FOX_WIDTH, F32), t_shape(FOX_WIDTH, F32), t_shape(LANES, F32)) + common_shapes
        out_specs = (row(d), row(FOX_WIDTH), t_spec(FOX_WIDTH), t_spec(FOX_WIDTH), t_spec(FOX_WIDTH),
                     t_spec(FOX_WIDTH), t_spec(LANES)) + common_specs
    return pl.pallas_call(
        functools.partial(_proj_in_kernel, positions_minor=seq is not None),
        grid=(ntiles,),
        in_specs=[pl.BlockSpec((tm, d), lambda i: (tile0 + i, 0)), _const_spec((1, d)), _const_spec(w_main.shape),
                  _const_spec(w_f.shape), _const_spec(b_f.shape), _const_spec((1, GM_WIDTH))],
        out_specs=out_specs,
        out_shape=out_shapes,
        compiler_params=_params("parallel"),
        name="proj_in_prompt" if seq is not None else "proj_in_sample",
    )(x, g, w_main, w_f, b_f, gm_g)


def _cumsum_kernel(lf_ref, o_ref, parts_ref, carry_ref):
    @pl.when(pl.program_id(1) == 0)
    def _():
        carry_ref[...] = jnp.zeros_like(carry_ref)

    tc = lf_ref.shape[-1]
    upper = (lax.broadcasted_iota(jnp.int32, (tc, tc), 0)
             <= lax.broadcasted_iota(jnp.int32, (tc, tc), 1)).astype(F32)
    y = jnp.dot(lf_ref[...], upper, precision=HIGHEST, preferred_element_type=F32) + carry_ref[...]
    carry_ref[...] = y[:, tc - 1:tc]
    y = y * LOG2E
    o_ref[...] = y
    hi = y.astype(BF16).astype(F32)
    mid = (y - hi).astype(BF16).astype(F32)
    parts_ref[0] = hi
    parts_ref[1] = mid
    parts_ref[2] = ((y - hi) - mid).astype(BF16).astype(F32)


def _cumsum_lanes(lf_t):
    b, h, s = lf_t.shape
    tc = min(CUMSUM_TILE, s)
    return pl.pallas_call(
        _cumsum_kernel,
        grid=(b, s // tc),
        in_specs=[pl.BlockSpec((None, h, tc), lambda i, j: (i, 0, j))],
        out_specs=(pl.BlockSpec((None, h, tc), lambda i, j: (i, 0, j)),
                   pl.BlockSpec((None, 3, h, tc), lambda i, j: (i, 0, 0, j))),
        out_shape=(jax.ShapeDtypeStruct((b, h, s), F32), jax.ShapeDtypeStruct((b, 3, h, s), F32)),
        scratch_shapes=[pltpu.VMEM((h, 1), F32)],
        compiler_params=_params("parallel", "arbitrary"),
        name="fox_cumsum",
    )(lf_t)


def _fox_attn_kernel(qi_tab, kj_tab, q_ref, k_ref, fa_ref, v_ref, fr_ref, o_ref,
                     qm_ref, fr_rep, m_ref, l_ref, alpha_ref, shift_ref, tmax_ref, acc_ref, s_ref, p_ref,
                     *, tq, tk):
    p = pl.program_id(2)
    qi = qi_tab[p]
    kj = kj_tab[p]
    lane = lax.broadcasted_iota(jnp.int32, (tq, LANES), 1)
    low = lane < FOX_HEAD_DIM
    n_col = tk // LANES
    n_chunk = tq // ATTN_ROWS

    @pl.when(kj == 0)
    def _():
        q = q_ref[...]
        for hh in range(2):
            qm_ref[hh, :, :LANES] = jnp.where(low if hh == 0 else jnp.logical_not(low), q, jnp.zeros_like(q))
            minus_one = (lane >= 3 * hh) & (lane < 3 * hh + 3)
            qm_ref[hh, :, LANES:] = jnp.where(minus_one, -1.0, 0.0).astype(BF16)
            fr_rep[hh] = jnp.broadcast_to(fr_ref[:, hh:hh + 1], (tq, LANES))
        m_ref[...] = jnp.full_like(m_ref, -jnp.inf)
        l_ref[...] = jnp.zeros_like(l_ref)
        acc_ref[...] = jnp.zeros_like(acc_ref)

    def step(masked):
        sub = lax.broadcasted_iota(jnp.int32, (ATTN_ROWS, LANES), 0)
        lane_r = lax.broadcasted_iota(jnp.int32, (ATTN_ROWS, LANES), 1)
        ka_t = jnp.concatenate([k_ref[...], fa_ref[...]], axis=0)
        for hh in range(2):
            s_ref[hh] = jnp.dot(qm_ref[hh], ka_t, preferred_element_type=F32)

        def scores(hh, r0):
            ts = []
            for j in range(n_col):
                t = s_ref[hh, pl.ds(r0, ATTN_ROWS), j * LANES:(j + 1) * LANES]
                if masked:
                    keep = (kj * tk + j * LANES + lane_r) <= (qi * tq + r0 + sub)
                    t = jnp.where(keep, t, -jnp.inf)
                ts.append(t)
            return ts

        for hh in range(2):
            def max_body(i, carry, hh=hh):
                r0 = pl.multiple_of(i * ATTN_ROWS, ATTN_ROWS)
                tmax_ref[hh, pl.ds(r0, ATTN_ROWS), :] = functools.reduce(jnp.maximum, scores(hh, r0))
                return carry

            lax.fori_loop(0, n_chunk, max_body, 0, unroll=ATTN_MAX_UNROLL)

        for hh in range(2):
            f_row = fr_rep[hh]
            m_prev = m_ref[hh]
            m_new = jnp.maximum(m_prev, jnp.max(tmax_ref[hh], axis=1, keepdims=True) + f_row)
            m_ref[hh] = m_new
            alpha_ref[hh] = jnp.exp2(m_prev - m_new)
            shift_ref[hh] = m_new - f_row

        v_t = v_ref[...]
        pvs = []
        for hh in range(2):
            def exp_body(i, carry, hh=hh):
                r0 = pl.multiple_of(i * ATTN_ROWS, ATTN_ROWS)
                rows = pl.ds(r0, ATTN_ROWS)
                shift = shift_ref[hh, rows, :]
                prs = [jnp.exp2(t - shift) for t in scores(hh, r0)]
                l_ref[hh, rows, :] = alpha_ref[hh, rows, :] * l_ref[hh, rows, :] + functools.reduce(jnp.add, prs)
                for j in range(n_col):
                    p_ref[hh, rows, j * LANES:(j + 1) * LANES] = prs[j].astype(BF16)
                return carry

            lax.fori_loop(0, n_chunk, exp_body, 0, unroll=ATTN_EXP_UNROLL)
            pvs.append(lax.dot_general(p_ref[hh], v_t, (((1,), (1,)), ((), ())), preferred_element_type=F32))
        acc = acc_ref[...]
        acc_ref[...] = jnp.where(low, alpha_ref[0] * acc + pvs[0], alpha_ref[1] * acc + pvs[1])

    on_diagonal = (kj + 1) * tk > qi * tq + 1
    pl.when(on_diagonal)(lambda: step(True))
    pl.when(jnp.logical_not(on_diagonal))(lambda: step(False))

    @pl.when((kj + 1) * tk >= (qi + 1) * tq)
    def _():
        l0 = jnp.sum(l_ref[0], axis=1, keepdims=True)
        l1 = jnp.sum(l_ref[1], axis=1, keepdims=True)
        o_ref[...] = (acc_ref[...] * jnp.where(low, 1.0 / l0, 1.0 / l1)).astype(BF16)


def _fox_prompt(q, kb_t, f_aug_t, vb_t, f_row, batch, seq):
    tq = min(ATTN_TQ, seq)
    tk = min(ATTN_TK, tq)
    nq = seq // tq
    pairs = [(i, j) for i in range(nq) for j in range(((i + 1) * tq) // tk)]
    qi_tab = jnp.asarray(np.array([a for a, _ in pairs], np.int32))
    kj_tab = jnp.asarray(np.array([c for _, c in pairs], np.int32))
    nqb = seq // tq
    grid_spec = pltpu.PrefetchScalarGridSpec(
        num_scalar_prefetch=2,
        grid=(batch, FOX_HEADS // 2, len(pairs)),
        in_specs=[
            pl.BlockSpec((tq, LANES), lambda b, h, p, qt, kt: (b * nqb + qt[p], h)),
            pl.BlockSpec((None, LANES, tk), lambda b, h, p, qt, kt: (b, h, kt[p])),
            pl.BlockSpec((None, None, LANES, tk), lambda b, h, p, qt, kt: (b, h, 0, kt[p])),
            pl.BlockSpec((None, LANES, tk), lambda b, h, p, qt, kt: (b, h, kt[p])),
            pl.BlockSpec((None, None, tq, 2), lambda b, h, p, qt, kt: (b, h, qt[p], 0)),
        ],
        out_specs=pl.BlockSpec((tq, LANES), lambda b, h, p, qt, kt: (b * nqb + qt[p], h)),
        scratch_shapes=[pltpu.VMEM((2, tq, 2 * LANES), BF16),
                        pltpu.VMEM((2, tq, LANES), F32),
                        pltpu.VMEM((2, tq, LANES), F32),
                        pltpu.VMEM((2, tq, LANES), F32),
                        pltpu.VMEM((2, tq, LANES), F32),
                        pltpu.VMEM((2, tq, LANES), F32),
                        pltpu.VMEM((2, tq, LANES), F32),
                        pltpu.VMEM((tq, LANES), F32),
                        pltpu.VMEM((2, tq, tk), F32),
                        pltpu.VMEM((2, tq, tk), BF16)],
    )
    return pl.pallas_call(
        functools.partial(_fox_attn_kernel, tq=tq, tk=tk),
        grid_spec=grid_spec,
        out_shape=jax.ShapeDtypeStruct((batch * seq, FOX_WIDTH), BF16),
        compiler_params=_params("parallel", "parallel", "arbitrary"),
        name="fox_prompt_attn",
    )(qi_tab, kj_tab, q, kb_t, f_aug_t, vb_t, f_row)


def _fox_sample_kernel(pt_ref, q_ref, kn_ref, vn_ref, lfn_ref, *refs, n_pages, t_new):
    k_refs = refs[:n_pages]
    v_refs = refs[n_pages:2 * n_pages]
    lf_refs = refs[2 * n_pages:3 * n_pages]
    o_ref = refs[3 * n_pages]
    t_ref = refs[3 * n_pages + 1]
    rows = FOX_HEADS * t_new
    nt = (((1,), (1,)), ((), ()))

    q = q_ref[...]
    qt = jnp.concatenate([q] * FOX_HEADS, axis=0)
    r_w = lax.broadcasted_iota(jnp.int32, (rows, FOX_WIDTH), 0)
    c_w = lax.broadcasted_iota(jnp.int32, (rows, FOX_WIDTH), 1)
    head_match = _div(r_w, t_new) == _div(c_w, FOX_HEAD_DIM)
    q_bd = jnp.where(head_match, qt, 0.0).astype(BF16)
    upper = (lax.broadcasted_iota(jnp.int32, (PAGE_SIZE, PAGE_SIZE), 0)
             <= lax.broadcasted_iota(jnp.int32, (PAGE_SIZE, PAGE_SIZE), 1)).astype(F32)
    r_p = lax.broadcasted_iota(jnp.int32, (rows, PAGE_SIZE), 0)
    c_p = lax.broadcasted_iota(jnp.int32, (rows, PAGE_SIZE), 1)
    pad = jnp.zeros((PAGE_SIZE - t_new, FOX_WIDTH), F32)

    carry = jnp.zeros((FOX_HEADS, 1), F32)
    m = jnp.full((rows, 1), -jnp.inf, F32)
    f_row = None
    for pg in range(n_pages + 1):
        if pg < n_pages:
            k_t = k_refs[pg][...].reshape(FOX_WIDTH, PAGE_SIZE).astype(BF16)
            s = jnp.dot(q_bd, k_t, preferred_element_type=F32)
            lf_t = lf_refs[pg][...]
        else:
            k_new = jnp.concatenate([kn_ref[...], pad], axis=0).astype(BF16)
            s = lax.dot_general(q_bd, k_new, nt, preferred_element_type=F32)
            lf_t = lfn_ref[...]
        f_t = jnp.dot(lf_t, upper, precision=HIGHEST, preferred_element_type=F32) + carry
        carry = f_t[:, PAGE_SIZE - 1:PAGE_SIZE]
        f_t = f_t * LOG2E
        f_col = jnp.concatenate(
            [jnp.broadcast_to(f_t[h:h + 1, :], (t_new, PAGE_SIZE)) for h in range(FOX_HEADS)], axis=0)
        t = s - f_col
        if pg == n_pages:
            t = jnp.where(c_p <= _mod(r_p, t_new), t, -jnp.inf)
            f_row = jnp.sum(jnp.where(c_p == _mod(r_p, t_new), f_col, 0.0), axis=1, keepdims=True)
        t_ref[pg] = t
        m = jnp.maximum(m, jnp.max(t, axis=1, keepdims=True))
    m_logit = m + f_row
    shift = m_logit - f_row
    l = jnp.zeros((rows, 1), F32)
    acc = jnp.zeros((rows, FOX_WIDTH), F32)
    for pg in range(n_pages + 1):
        pr = jnp.exp2(t_ref[pg] - shift)
        l = l + jnp.sum(pr, axis=1, keepdims=True)
        if pg < n_pages:
            v_t = v_refs[pg][...].reshape(FOX_WIDTH, PAGE_SIZE).astype(BF16)
            acc = acc + lax.dot_general(pr.astype(BF16), v_t, nt, preferred_element_type=F32)
        else:
            v_new = jnp.concatenate([vn_ref[...], pad], axis=0).astype(BF16)
            acc = acc + jnp.dot(pr.astype(BF16), v_new, preferred_element_type=F32)
    o = acc / l
    c_o = lax.broadcasted_iota(jnp.int32, (t_new, FOX_WIDTH), 1)
    out = jnp.zeros((t_new, FOX_WIDTH), F32)
    for h in range(FOX_HEADS):
        out = jnp.where(_div(c_o, FOX_HEAD_DIM) == h, o[h * t_new:(h + 1) * t_new, :], out)
    o_ref[...] = out


def _fox_sample(page_table, q32, k32, v32, lf_new_t, cache_kt, cache_vt, cache_lf_t, layer, row0, t_new):
    nb, n_pages = page_table.shape
    blk0 = row0 // t_new
    tok = pl.BlockSpec((t_new, FOX_WIDTH), lambda b, pt: (blk0 + b, 0))

    def kv_spec(pg):
        return pl.BlockSpec((None, None, FOX_HEADS, FOX_HEAD_DIM, PAGE_SIZE),
                            lambda b, pt: (layer, pt[b, pg], 0, 0, 0))

    def lf_spec(pg):
        return pl.BlockSpec((None, None, FOX_HEADS, PAGE_SIZE), lambda b, pt: (layer, pt[b, pg], 0, 0))

    in_specs = [tok, tok, tok, pl.BlockSpec((None, FOX_HEADS, LANES), lambda b, pt: (b, 0, 0))]
    in_specs += [kv_spec(pg) for pg in range(n_pages)] * 2
    in_specs += [lf_spec(pg) for pg in range(n_pages)]
    grid_spec = pltpu.PrefetchScalarGridSpec(
        num_scalar_prefetch=1,
        grid=(nb,),
        in_specs=in_specs,
        out_specs=pl.BlockSpec((t_new, FOX_WIDTH), lambda b, pt: (b, 0)),
        scratch_shapes=[pltpu.VMEM((n_pages + 1, FOX_HEADS * t_new, PAGE_SIZE), F32)],
    )
    return pl.pallas_call(
        functools.partial(_fox_sample_kernel, n_pages=n_pages, t_new=t_new),
        grid_spec=grid_spec,
        out_shape=jax.ShapeDtypeStruct((nb * t_new, FOX_WIDTH), F32),
        compiler_params=_params("parallel"),
        name="fox_sample_attn",
    )(page_table, q32, k32, v32, lf_new_t, *([cache_kt] * n_pages), *([cache_vt] * n_pages),
      *([cache_lf_t] * n_pages))


def _norm_matmul_kernel(x_ref, g_ref, w_ref, o_ref):
    hb = _rmsnorm(x_ref[...], g_ref[...]).astype(BF16)
    o_ref[...] = jnp.dot(hb, w_ref[...], preferred_element_type=F32)


def _norm_matmul(x, g, w):
    n, d = x.shape
    tm = min(TOKEN_TILE, n)
    return pl.pallas_call(
        _norm_matmul_kernel,
        grid=(n // tm,),
        in_specs=[pl.BlockSpec((tm, d), lambda i: (i, 0)), _const_spec((1, d)), _const_spec(w.shape)],
        out_specs=pl.BlockSpec((tm, w.shape[1]), lambda i: (i, 0)),
        out_shape=jax.ShapeDtypeStruct((n, w.shape[1]), F32),
        compiler_params=_params("parallel"),
        name="mem_kv",
    )(x, g, w)


def _mem_attend_kernel(q_ref, mk_ref, mv_ref, o_ref):
    scale = MEM_HEAD_DIM ** -0.5
    n_mem = mk_ref.shape[0]
    t = q_ref.shape[0] // n_mem
    for g in range(n_mem):
        rs = slice(g * t, (g + 1) * t)
        for h in range(MEM_HEADS):
            sl = slice(h * MEM_HEAD_DIM, (h + 1) * MEM_HEAD_DIM)
            s = lax.dot_general(q_ref[rs, sl].astype(BF16), mk_ref[g, :, sl].astype(BF16),
                                (((1,), (1,)), ((), ())), preferred_element_type=F32) * scale
            e = jnp.exp(s - jnp.max(s, axis=1, keepdims=True))
            pr = e / jnp.sum(e, axis=1, keepdims=True)
            o_ref[rs, sl] = jnp.dot(pr.astype(BF16), mv_ref[g, :, sl].astype(BF16), preferred_element_type=F32)


def _mem_attend_few_rows_kernel(q_ref, mk_ref, mv_ref, o_ref):
    scale = MEM_HEAD_DIM ** -0.5
    n_mem, m_len = mk_ref.shape[0], mk_ref.shape[1]
    t = q_ref.shape[0] // n_mem
    rows, width = MEM_HEADS * t, m_len * MEM_HEADS
    r = lax.broadcasted_iota(jnp.int32, (rows, width), 0)
    c = lax.broadcasted_iota(jnp.int32, (rows, width), 1)
    valid = _mod(c, MEM_HEADS) == _div(r, t)
    for g in range(n_mem):
        rs = slice(g * t, (g + 1) * t)
        q_ht = jnp.concatenate([q_ref[rs, h * MEM_HEAD_DIM:(h + 1) * MEM_HEAD_DIM] for h in range(MEM_HEADS)],
                               axis=0).astype(BF16)
        mk2 = mk_ref[g].reshape(width, MEM_HEAD_DIM).astype(BF16)
        s = lax.dot_general(q_ht, mk2, (((1,), (1,)), ((), ())), preferred_element_type=F32) * scale
        s = jnp.where(valid, s, -jnp.inf)
        e = jnp.exp(s - jnp.max(s, axis=1, keepdims=True))
        pr = e / jnp.sum(e, axis=1, keepdims=True)
        o = jnp.dot(pr.astype(BF16), mv_ref[g].reshape(width, MEM_HEAD_DIM).astype(BF16),
                    preferred_element_type=F32)
        for h in range(MEM_HEADS):
            o_ref[rs, h * MEM_HEAD_DIM:(h + 1) * MEM_HEAD_DIM] = o[h * t:(h + 1) * t, :]


def _mem_attend(mq, mk, mv, row0, n_mem, t_per_mem, name, layer=None):
    tm = min(TOKEN_TILE * 2, t_per_mem)
    tiles = t_per_mem // tm
    group = max(1, min(MEM_GROUP_ROWS // tm, n_mem)) if tiles == 1 else 1
    assert n_mem % group == 0
    rows = tm * group
    blk0 = row0 // rows
    if layer is None:
        assert group == 1
        mem_spec = pl.BlockSpec((1, mk.shape[1], MEM_WIDTH), lambda b, i: (b, 0, 0))
    else:
        assert group > 1
        mem_spec = pl.BlockSpec((None, group, mk.shape[2], MEM_HEADS, MEM_HEAD_DIM),
                                lambda b, i: (layer, b, 0, 0, 0))
    return pl.pallas_call(
        _mem_attend_few_rows_kernel if group > 1 else _mem_attend_kernel,
        grid=(n_mem // group, tiles),
        in_specs=[pl.BlockSpec((rows, MEM_WIDTH), lambda b, i: (blk0 + b * tiles + i, 0)), mem_spec, mem_spec],
        out_specs=pl.BlockSpec((rows, MEM_WIDTH), lambda b, i: (b * tiles + i, 0)),
        out_shape=jax.ShapeDtypeStruct((n_mem * t_per_mem, MEM_WIDTH), F32),
        compiler_params=_params("parallel", "parallel"),
        name=name,
    )(mq, mk, mv)


def _merge_kernel(x_ref, h_ref, fox_ref, u_ref, vn_ref, mem_ref, ws_ref, bs_ref,
                  wg_ref, bg_ref, wbr_ref, wo_ref, o_ref, gm_ref, *, chunk):
    tm, d = x_ref.shape
    r_c = lax.broadcasted_iota(jnp.int32, (CHUNK, CHUNK), 0)
    c_c = lax.broadcasted_iota(jnp.int32, (CHUNK, CHUNK), 1)
    keep = (c_c <= r_c) & (_div(r_c, chunk) == _div(c_c, chunk))
    for g in range(GM_GROUPS):
        w = jnp.where(keep, ws_ref[g], 0.0).astype(BF16)
        sl = slice(g * GM_GROUP_DIM, (g + 1) * GM_GROUP_DIM)
        for c in range(tm // CHUNK):
            rs = slice(c * CHUNK, (c + 1) * CHUNK)
            mix = jnp.dot(w, vn_ref[rs, sl].astype(BF16), preferred_element_type=F32) + bs_ref[:, g:g + 1]
            gm_ref[rs, sl] = (u_ref[rs, sl] * mix).astype(BF16)

    hb = h_ref[...]
    branches = (fox_ref[...].astype(BF16), gm_ref[...], mem_ref[...].astype(BF16))
    merged = None
    for r in range(N_BRANCH):
        gate = jax.nn.sigmoid(jnp.dot(hb, wg_ref[:, r * d:(r + 1) * d], preferred_element_type=F32)
                              + bg_ref[:, r * d:(r + 1) * d])
        term = gate * jnp.dot(branches[r], wbr_ref[r], preferred_element_type=F32)
        merged = term if merged is None else merged + term
    o_ref[...] = x_ref[...] + jnp.dot(merged.astype(BF16), wo_ref[...], preferred_element_type=F32)


def _merge(x, x_tile0, h, fox_o, u, vn, mem_o, ws, bs_t, w_gate, b_gate, w_br, w_o, ntiles, chunk, name):
    d = x.shape[1]
    tm = TOKEN_TILE
    own = lambda width: pl.BlockSpec((tm, width), lambda i: (i, 0))
    in_specs = [pl.BlockSpec((tm, d), lambda i: (x_tile0 + i, 0)), own(d), own(FOX_WIDTH), own(GM_WIDTH),
                own(GM_WIDTH), own(MEM_WIDTH), _const_spec(ws.shape), _const_spec(bs_t.shape),
                _const_spec(w_gate.shape), _const_spec((1, N_BRANCH * d)), _const_spec(w_br.shape),
                _const_spec(w_o.shape)]
    return pl.pallas_call(
        functools.partial(_merge_kernel, chunk=chunk),
        grid=(ntiles,),
        in_specs=in_specs,
        out_specs=pl.BlockSpec((tm, d), lambda i: (i, 0)),
        out_shape=jax.ShapeDtypeStruct((ntiles * tm, d), F32),
        scratch_shapes=[pltpu.VMEM((tm, GM_WIDTH), BF16)],
        compiler_params=_params("parallel"),
        name=name,
    )(x, h, fox_o, u, vn, mem_o, ws, bs_t, w_gate, b_gate, w_br, w_o)


def _moe_route_kernel(x_ref, g_ref, wr_ref, br_ref, hn_ref, meta_ref, cnt_ref, carry_ref):
    @pl.when(pl.program_id(0) == 0)
    def _():
        carry_ref[...] = jnp.zeros_like(carry_ref)

    tm = x_ref.shape[0]
    hn = _rmsnorm(x_ref[...], g_ref[...])
    hn_ref[...] = hn
    logits = jnp.dot(hn, wr_ref[...], precision=HIGHEST, preferred_element_type=F32) + br_ref[...]
    col = lax.broadcasted_iota(jnp.int32, (tm, LANES), 1)
    colf = col.astype(F32)
    big = jnp.float32(1e9)

    def masked_softmax(mask):
        z = jnp.where(mask, logits, -jnp.inf)
        e = jnp.exp(z - jnp.max(z, axis=1, keepdims=True))
        return e / jnp.sum(e, axis=1, keepdims=True)

    def first_argmax(vals, vmax, mask):
        return jnp.min(jnp.where(mask & (vals == vmax), colf, big), axis=1, keepdims=True)

    is_group = col < N_GROUPS
    g_prob = masked_softmax(is_group)
    p_g = jnp.max(g_prob, axis=1, keepdims=True)
    g_idx = first_argmax(g_prob, p_g, is_group)
    e_col = col - N_GROUPS
    in_group = (e_col >= 0) & (e_col < N_EXPERTS) & (_div(e_col, EXPERTS_PER_GROUP).astype(F32) == g_idx)
    e_prob = masked_softmax(in_group)
    p1 = jnp.max(e_prob, axis=1, keepdims=True)
    i1 = first_argmax(e_prob, p1, in_group)
    rest = in_group & (colf != i1)
    p2 = jnp.max(jnp.where(rest, e_prob, -1.0), axis=1, keepdims=True)
    i2 = first_argmax(e_prob, p2, rest)
    denom = p1 + p2
    w1 = p1 / denom * p_g
    w2 = p2 / denom * p_g
    e1 = i1 - N_GROUPS
    e2 = i2 - N_GROUPS

    onehot = ((colf == e1) | (colf == e2)).astype(F32)
    lower = (lax.broadcasted_iota(jnp.int32, (tm, tm), 1)
             < lax.broadcasted_iota(jnp.int32, (tm, tm), 0)).astype(BF16)
    before = jnp.dot(lower, onehot.astype(BF16), preferred_element_type=F32) + carry_ref[...]
    r1 = jnp.sum(jnp.where(colf == e1, before, 0.0), axis=1, keepdims=True)
    r2 = jnp.sum(jnp.where(colf == e2, before, 0.0), axis=1, keepdims=True)
    carry_ref[...] = carry_ref[...] + jnp.sum(onehot, axis=0, keepdims=True)
    cnt_ref[...] = jnp.broadcast_to(carry_ref[...], cnt_ref.shape)

    meta = jnp.zeros((tm, LANES), F32)
    for lane_id, val in enumerate((e1, e2, r1, r2, w1, w2)):
        meta = jnp.where(col == lane_id, val, meta)
    meta_ref[...] = meta


def _moe_route(x, g, w_route, b_route):
    n, d = x.shape
    tm = ROUTE_TILE
    return pl.pallas_call(
        _moe_route_kernel,
        grid=(n // tm,),
        in_specs=[pl.BlockSpec((tm, d), lambda i: (i, 0)), _const_spec((1, d)),
                  _const_spec(w_route.shape), _const_spec((1, LANES))],
        out_specs=(pl.BlockSpec((tm, d), lambda i: (i, 0)), pl.BlockSpec((tm, LANES), lambda i: (i, 0)),
                   _const_spec((8, LANES))),
        out_shape=(jax.ShapeDtypeStruct((n, d), F32), jax.ShapeDtypeStruct((n, LANES), F32),
                   jax.ShapeDtypeStruct((8, LANES), F32)),
        scratch_shapes=[pltpu.VMEM((1, LANES), F32)],
        compiler_params=_params("arbitrary"),
        name="moe_route",
    )(x, g, w_route, b_route)


def _moe_expert_kernel(te_ref, tv_ref, src_ref, src_next_ref, hn_hbm, wgu_ref, wdn_ref, ys_ref,
                       x_buf, wgu_bf, wdn_bf, sem, *, tg):
    t = pl.program_id(0)
    n_t = pl.num_programs(0)
    slot = t % 2

    def gather(idx_ref, buf_slot):
        def issue(i, c):
            base = pl.multiple_of(i * DMA_UNROLL, DMA_UNROLL)
            for k in range(DMA_UNROLL):
                pltpu.make_async_copy(hn_hbm.at[pl.ds(idx_ref[0, base + k], 1)],
                                      x_buf.at[buf_slot, pl.ds(base + k, 1)], sem.at[buf_slot]).start()
            return c

        lax.fori_loop(0, tg // DMA_UNROLL, issue, 0)

    @pl.when((t == 0) & (tv_ref[0] == 1))
    def _():
        gather(src_ref, 0)

    @pl.when((t + 1 < n_t) & (tv_ref[jnp.minimum(t + 1, n_t - 1)] == 1))
    def _():
        gather(src_next_ref, 1 - slot)

    @pl.when(tv_ref[t] == 1)
    def _():
        @pl.when((t == 0) | (te_ref[t] != te_ref[jnp.maximum(t - 1, 0)]))
        def _():
            wgu_bf[...] = wgu_ref[...].astype(BF16)
            wdn_bf[...] = wdn_ref[...].astype(BF16)

        pltpu.make_async_copy(hn_hbm.at[pl.ds(0, tg)], x_buf.at[slot], sem.at[slot]).wait()
        gu = jnp.dot(x_buf[slot].astype(BF16), wgu_bf[...], preferred_element_type=F32)
        gate = gu[:, :EXPERT_FF]
        act = gate * jax.nn.sigmoid(gate) * gu[:, EXPERT_FF:]
        ys_ref[...] = jnp.dot(act.astype(BF16), wdn_bf[...], preferred_element_type=F32)

    @pl.when(tv_ref[t] != 1)
    def _():
        ys_ref[...] = jnp.zeros_like(ys_ref)


def _moe_experts(tile_expert, tile_valid, src_idx, hn, w_gu, w_dn, layer):
    n_tiles = tile_expert.shape[0]
    tg = MOE_TILE
    d = hn.shape[1]
    grid_spec = pltpu.PrefetchScalarGridSpec(
        num_scalar_prefetch=2,
        grid=(n_tiles,),
        in_specs=[
            pl.BlockSpec((None, 1, tg), lambda t, te, tv: (t, 0, 0), memory_space=pltpu.SMEM),
            pl.BlockSpec((None, 1, tg), lambda t, te, tv: (jnp.minimum(t + 1, n_tiles - 1), 0, 0),
                         memory_space=pltpu.SMEM),
            pl.BlockSpec(memory_space=pl.ANY),
            pl.BlockSpec((None, None, d, 2 * EXPERT_FF), lambda t, te, tv: (layer, te[t], 0, 0)),
            pl.BlockSpec((None, None, EXPERT_FF, d), lambda t, te, tv: (layer, te[t], 0, 0)),
        ],
        out_specs=pl.BlockSpec((tg, d), lambda t, te, tv: (t, 0)),
        scratch_shapes=[pltpu.VMEM((2, tg, d), F32), pltpu.VMEM((d, 2 * EXPERT_FF), BF16),
                        pltpu.VMEM((EXPERT_FF, d), BF16), pltpu.SemaphoreType.DMA((2,))],
    )
    return pl.pallas_call(
        functools.partial(_moe_expert_kernel, tg=tg),
        grid_spec=grid_spec,
        out_shape=jax.ShapeDtypeStruct((n_tiles * tg, d), F32),
        compiler_params=_params("arbitrary"),
        name="moe_experts",
    )(tile_expert, tile_valid, src_idx, src_idx, hn, w_gu, w_dn)


def _moe_combine_kernel(dest_ref, x_ref, meta_ref, fg_ref, ys_hbm, o_ref, buf, sem, *, tm, final):
    def issue(i, c):
        base = pl.multiple_of(i * DMA_UNROLL, DMA_UNROLL)
        for k in range(DMA_UNROLL):
            for slot in range(2):
                pltpu.make_async_copy(ys_hbm.at[pl.ds(dest_ref[0, 2 * (base + k) + slot], 1)],
                                      buf.at[slot, pl.ds(base + k, 1)], sem.at[slot]).start()
        return c

    lax.fori_loop(0, tm // DMA_UNROLL, issue, 0)
    for slot in range(2):
        pltpu.make_async_copy(ys_hbm.at[pl.ds(0, tm)], buf.at[slot], sem.at[slot]).wait()
    meta = meta_ref[...]
    y = x_ref[...] + (meta[:, 4:5] * buf[0] + meta[:, 5:6] * buf[1])
    if final:
        y = _rmsnorm(y, fg_ref[...])
    o_ref[...] = y


def _moe_combine(dest, x, meta, final_g, ys, final, tile0, ntiles):
    d = x.shape[1]
    tm = TOKEN_TILE
    return pl.pallas_call(
        functools.partial(_moe_combine_kernel, tm=tm, final=final),
        grid=(ntiles,),
        in_specs=[pl.BlockSpec((None, 1, 2 * tm), lambda i: (tile0 + i, 0, 0), memory_space=pltpu.SMEM),
                  pl.BlockSpec((tm, d), lambda i: (tile0 + i, 0)),
                  pl.BlockSpec((tm, LANES), lambda i: (tile0 + i, 0)),
                  _const_spec((1, d)), pl.BlockSpec(memory_space=pl.ANY)],
        out_specs=pl.BlockSpec((tm, d), lambda i: (i, 0)),
        out_shape=jax.ShapeDtypeStruct((ntiles * tm, d), F32),
        scratch_shapes=[pltpu.VMEM((2, tm, d), F32), pltpu.SemaphoreType.DMA((2,))],
        compiler_params=_params("arbitrary"),
        name="moe_combine",
    )(dest, x, meta, final_g, ys)


def _moe_scatter_kernel(dest_ref, hn_ref, xs_in, xs_hbm, sem, *, tm):
    def issue(i, c):
        base = pl.multiple_of(i * DMA_UNROLL, DMA_UNROLL)
        for k in range(DMA_UNROLL):
            for slot in range(2):
                pltpu.make_async_copy(hn_ref.at[pl.ds(base + k, 1)],
                                      xs_hbm.at[pl.ds(dest_ref[0, 2 * (base + k) + slot], 1)],
                                      sem.at[slot]).start()
        return c

    lax.fori_loop(0, tm // DMA_UNROLL, issue, 0)
    for slot in range(2):
        pltpu.make_async_copy(hn_ref, xs_hbm.at[pl.ds(0, tm)], sem.at[slot]).wait()


def _moe_scatter(dest_tiles, hn, n_rows):
    n, d = hn.shape
    tm = TOKEN_TILE
    return pl.pallas_call(
        functools.partial(_moe_scatter_kernel, tm=tm),
        grid=(n // tm,),
        in_specs=[pl.BlockSpec((None, 1, 2 * tm), lambda i: (i, 0, 0), memory_space=pltpu.SMEM),
                  pl.BlockSpec((tm, d), lambda i: (i, 0)), pl.BlockSpec(memory_space=pl.ANY)],
        out_specs=pl.BlockSpec(memory_space=pl.ANY),
        out_shape=jax.ShapeDtypeStruct((n_rows, d), F32),
        input_output_aliases={2: 0},
        scratch_shapes=[pltpu.SemaphoreType.DMA((2,))],
        compiler_params=_params("arbitrary"),
        name="moe_scatter",
    )(dest_tiles, hn, jnp.zeros((n_rows, d), F32))


def _moe_expert_sorted_kernel(te_ref, tv_ref, x_ref, wgu_ref, wdn_ref, ys_ref, wgu_bf, wdn_bf):
    t = pl.program_id(0)

    @pl.when(tv_ref[t] == 1)
    def _():
        @pl.when((t == 0) | (te_ref[t] != te_ref[jnp.maximum(t - 1, 0)]))
        def _():
            wgu_bf[...] = wgu_ref[...].astype(BF16)
            wdn_bf[...] = wdn_ref[...].astype(BF16)

        gu = jnp.dot(x_ref[...].astype(BF16), wgu_bf[...], preferred_element_type=F32)
        gate = gu[:, :EXPERT_FF]
        act = gate * jax.nn.sigmoid(gate) * gu[:, EXPERT_FF:]
        ys_ref[...] = jnp.dot(act.astype(BF16), wdn_bf[...], preferred_element_type=F32)

    @pl.when(tv_ref[t] != 1)
    def _():
        ys_ref[...] = jnp.zeros_like(ys_ref)


def _moe_experts_sorted(tile_expert, tile_valid, xs, w_gu, w_dn, layer):
    n_tiles = tile_expert.shape[0]
    tg = MOE_TILE
    d = xs.shape[1]
    grid_spec = pltpu.PrefetchScalarGridSpec(
        num_scalar_prefetch=2,
        grid=(n_tiles,),
        in_specs=[
            pl.BlockSpec((tg, d), lambda t, te, tv: (t, 0)),
            pl.BlockSpec((None, None, d, 2 * EXPERT_FF), lambda t, te, tv: (layer, te[t], 0, 0)),
            pl.BlockSpec((None, None, EXPERT_FF, d), lambda t, te, tv: (layer, te[t], 0, 0)),
        ],
        out_specs=pl.BlockSpec((tg, d), lambda t, te, tv: (t, 0)),
        scratch_shapes=[pltpu.VMEM((d, 2 * EXPERT_FF), BF16), pltpu.VMEM((EXPERT_FF, d), BF16)],
    )
    return pl.pallas_call(
        _moe_expert_sorted_kernel,
        grid_spec=grid_spec,
        out_shape=jax.ShapeDtypeStruct((n_tiles * tg, d), F32),
        compiler_params=_params("arbitrary"),
        name="moe_experts_sorted",
    )(tile_expert, tile_valid, xs, w_gu, w_dn)


def _moe_layer(x, g, w_route, b_route, w_gu, w_dn, layer, final_g, final, splits):
    n, d = x.shape
    tg = MOE_TILE
    hn, meta, counts = _moe_route(x, g, w_route, b_route)
    counts = counts[0, :N_EXPERTS].astype(jnp.int32)
    padded = ((counts + tg - 1) // tg) * tg
    ends = jnp.cumsum(padded)
    starts = ends - padded
    eid = meta[:, 0:2].astype(jnp.int32)
    rank = meta[:, 2:4].astype(jnp.int32)
    dest = starts[eid] + rank
    n_tiles = (2 * n) // tg + N_EXPERTS
    tile_start = jnp.arange(n_tiles, dtype=jnp.int32) * tg
    tile_expert = jnp.minimum(jnp.sum((ends[None, :] <= tile_start[:, None]).astype(jnp.int32), axis=1),
                              N_EXPERTS - 1)
    tile_valid = (tile_start < ends[-1]).astype(jnp.int32)
    dest_tiles = dest.reshape(n // TOKEN_TILE, 1, 2 * TOKEN_TILE)
    xs = _moe_scatter(dest_tiles, hn, n_tiles * tg)
    ys = _moe_experts_sorted(tile_expert, tile_valid, xs, w_gu, w_dn, layer)
    return tuple(_moe_combine(dest_tiles, x, meta, final_g, ys, final, t0, nt) for t0, nt in splits)


def kernel(x_prompt, x_sample, cache_k, cache_v, cache_logf, cache_mem_k, cache_mem_v, page_table, mem_prompt, norm1_g, w_in, b_f, gm_norm_g, gm_ws, gm_bs, mem_norm_g, w_mem_kv, w_gate, b_gate, w_br, w_o, norm2_g, w_rg, b_rg, w_re, b_re, w_moe_gu, w_moe_dn, final_g):
    batch, seq, d = x_prompt.shape
    dec_batch, dec_seq, _ = x_sample.shape
    depth = w_in.shape[0]
    n_pool = cache_k.shape[1]
    mem_len = mem_prompt.shape[1]
    n_p = batch * seq
    n_s = dec_batch * dec_seq
    assert n_p % (2 * TOKEN_TILE) == 0 and n_s % (2 * TOKEN_TILE) == 0 and seq % CHUNK == 0
    assert CHUNK % dec_seq == 0 and (n_p + n_s) % ROUTE_TILE == 0
    n_prompt_tiles = n_p // TOKEN_TILE

    n_sample_tiles = n_s // TOKEN_TILE
    n_all = n_p + n_s
    x_of = {"prompt": (x_prompt.reshape(n_p, d), 0), "sample": (x_sample.reshape(n_s, d), 0)}
    mem_flat = mem_prompt.reshape(batch * mem_len, d)
    ckt = jnp.transpose(cache_k, (0, 1, 3, 4, 2))
    cvt = jnp.transpose(cache_v, (0, 1, 3, 4, 2))
    clf_t = jnp.swapaxes(cache_logf, 2, 3)

    q_end, f_end = 3 * FOX_WIDTH, 3 * FOX_WIDTH + FOX_HEADS
    reps = CHUNK // dec_seq
    outs = {name: [] for name in ("kp", "vp", "fp", "mkp", "mvp", "ks", "vs", "fs", "gs")}
    for l in range(depth):
        w_main = jnp.concatenate([w_in[l][:, :q_end], w_in[l][:, f_end:]], axis=1).astype(BF16)
        w_f = jnp.pad(w_in[l][:, q_end:f_end], ((0, 0), (0, LANES - FOX_HEADS))).astype(BF16)
        bf_pad = jnp.pad(b_f[l], (0, LANES - FOX_HEADS)).reshape(1, LANES)
        ws_sample = jnp.tile(gm_ws[l][:, :dec_seq, :dec_seq], (1, reps, reps))
        bs_sample_t = jnp.tile(gm_bs[l][:, :dec_seq], (1, reps)).T
        w_route = jnp.pad(jnp.concatenate([w_rg[l], w_re[l]], axis=1),
                          ((0, 0), (0, LANES - N_GROUPS - N_EXPERTS)))
        b_route = jnp.pad(jnp.concatenate([b_rg[l], b_re[l]]), (0, LANES - N_GROUPS - N_EXPERTS)).reshape(1, LANES)

        w_kvf_t = jnp.concatenate([w_main[:, FOX_WIDTH:3 * FOX_WIDTH], w_f], axis=1).T
        norm_g, gm_g = norm1_g[l].reshape(1, d), gm_norm_g[l].reshape(1, GM_WIDTH)

        h_p, q_p, kb_t, vb_t, k_t, v_t, lf_t, u_p, vn_p, mq_p = _proj_in(
            x_of["prompt"][0], norm_g, w_main, w_kvf_t, bf_pad.reshape(LANES, 1), gm_g, x_of["prompt"][1],
            n_prompt_tiles, seq=seq)
        h_s, q_s, k_s, v_s, lf_s, u_s, vn_s, mq_s = _proj_in(
            x_of["sample"][0], norm_g, w_main, w_f, bf_pad, gm_g, x_of["sample"][1], n_sample_tiles)

        lf_p_t = lf_t[:, :FOX_HEADS, :]
        f_t, f_parts = _cumsum_lanes(lf_p_t)
        f_row = jnp.swapaxes(f_t.reshape(batch, FOX_HEADS // 2, 2, seq), 2, 3)
        f_aug_t = jnp.swapaxes(f_parts, 1, 2).reshape(batch, FOX_HEADS // 2, 6, seq)
        f_aug_t = jnp.pad(f_aug_t, ((0, 0), (0, 0), (0, LANES - 6), (0, 0))).astype(BF16)
        fox_p = _fox_prompt(q_p, kb_t, f_aug_t, vb_t, f_row, batch, seq)

        lf_s = lf_s[:, :FOX_HEADS].reshape(dec_batch, dec_seq, FOX_HEADS)
        lf_new_t = jnp.pad(jnp.swapaxes(lf_s, 1, 2), ((0, 0), (0, 0), (0, LANES - dec_seq)))
        fox_s = _fox_sample(page_table, q_s.astype(F32), k_s, v_s, lf_new_t, ckt, cvt, clf_t, l, 0, dec_seq)

        kv = _norm_matmul(mem_flat, mem_norm_g[l].reshape(1, d), w_mem_kv[l].astype(BF16))
        mk_p = kv[:, :MEM_WIDTH].reshape(batch, mem_len, MEM_WIDTH)
        mv_p = kv[:, MEM_WIDTH:].reshape(batch, mem_len, MEM_WIDTH)
        mem_p = _mem_attend(mq_p, mk_p, mv_p, 0, batch, seq, "mem_attend_prompt")
        mem_s = _mem_attend(mq_s, cache_mem_k, cache_mem_v, 0, dec_batch, dec_seq, "mem_attend_sample", layer=l)

        merge_w = (w_gate[l].astype(BF16), b_gate[l].reshape(1, N_BRANCH * d), w_br[l].astype(BF16),
                   w_o[l].astype(BF16))
        x_p = _merge(*x_of["prompt"], h_p, fox_p, u_p, vn_p, mem_p, gm_ws[l], gm_bs[l].T, *merge_w,
                     n_prompt_tiles, CHUNK, "merge_prompt")
        x_s = _merge(*x_of["sample"], h_s, fox_s, u_s, vn_s, mem_s, ws_sample, bs_sample_t, *merge_w,
                     n_sample_tiles, dec_seq, "merge_sample")
        x_mid = jnp.concatenate([x_p, x_s], axis=0)
        last = l == depth - 1
        splits = ((0, n_prompt_tiles), (n_prompt_tiles, n_sample_tiles)) if last else ((0, n_all // TOKEN_TILE),)
        x_out = _moe_layer(x_mid, norm2_g[l].reshape(1, d), w_route, b_route, w_moe_gu, w_moe_dn, l,
                           final_g.reshape(1, d), last, splits)
        if last:
            x_of = {"prompt": (x_out[0], 0), "sample": (x_out[1], 0)}
        else:
            x_of = {"prompt": (x_out[0], 0), "sample": (x_out[0], n_prompt_tiles)}

        to_bshd = lambda a: jnp.transpose(a.reshape(batch, FOX_HEADS, FOX_HEAD_DIM, seq), (0, 3, 1, 2))
        outs["kp"].append(to_bshd(k_t))
        outs["vp"].append(to_bshd(v_t))
        outs["fp"].append(jnp.swapaxes(lf_p_t, 1, 2))
        outs["mkp"].append(mk_p.reshape(batch, mem_len, MEM_HEADS, MEM_HEAD_DIM))
        outs["mvp"].append(mv_p.reshape(batch, mem_len, MEM_HEADS, MEM_HEAD_DIM))
        outs["ks"].append(k_s.reshape(dec_batch, dec_seq, FOX_HEADS, FOX_HEAD_DIM))
        outs["vs"].append(v_s.reshape(dec_batch, dec_seq, FOX_HEADS, FOX_HEAD_DIM))
        outs["fs"].append(lf_s)
        outs["gs"].append(vn_s.reshape(dec_batch, dec_seq, GM_WIDTH))

    y_prompt = x_of["prompt"][0].reshape(batch, seq, d)
    y_sample = x_of["sample"][0].reshape(dec_batch, dec_seq, d)
    return (y_prompt, y_sample) + tuple(
        jnp.stack(outs[name]) for name in ("kp", "vp", "fp", "mkp", "mvp", "ks", "vs", "fs", "gs"))
```
